```python
import jax, jax.numpy as jnp
from jax import lax
import numpy as np

D_MODEL = 1024
BATCH = 32
SEQ = 2048
DEPTH = 2

N_A_LAYERS = DEPTH // 2
N_B_LAYERS = DEPTH - N_A_LAYERS

SSD_EXPAND = 2
D_INNER = SSD_EXPAND * D_MODEL
SSD_HEADDIM = 64
SSD_HEADS = D_INNER // SSD_HEADDIM
SSD_GROUPS = 4
SSD_HPG = SSD_HEADS // SSD_GROUPS
SSD_STATE = 128
SSD_CONV = 4
SSD_CHUNK = 128
CONV_DIM = D_INNER + 2 * SSD_GROUPS * SSD_STATE
SSD_IN_DIM = D_INNER + CONV_DIM + SSD_HEADS

SB_HEADS = 16
SB_HEADDIM = D_MODEL // SB_HEADS
SB_QBLOCK = 128

N_EXPERTS = 32
TOP_K = 4
D_FF = D_MODEL
SWIGLU_LIMIT = 7.0
SWIGLU_ALPHA = 1.702

NORM_EPS = 1e-6

kernel_name = 'hybrid_ssd_stickbreaking_moe_yoco'


def _rmsnorm(x, g):
    xf = x.astype(jnp.float32)
    y = xf * lax.rsqrt(jnp.mean(xf * xf, axis=-1, keepdims=True) + NORM_EPS)
    return (y * g.astype(jnp.float32)).astype(x.dtype)


def _modulate(x, g, shift, scale):
    return _rmsnorm(x, g) * (1.0 + scale[:, None, :]) + shift[:, None, :]


def _causal_depthwise_conv(u, w, b):
    out = lax.conv_general_dilated(
        u, w[:, None, :].astype(u.dtype), window_strides=(1,),
        padding=[(SSD_CONV - 1, 0)], dimension_numbers=('NWC', 'WIO', 'NWC'),
        feature_group_count=u.shape[-1])
    return out + b.astype(u.dtype)


def _ssd_mixer(h, in_w, conv_w, conv_b, dt_bias, a_log, d_skip, norm_g, out_w):
    bsz, seq, _ = h.shape
    nc = seq // SSD_CHUNK
    zxbcdt = h @ in_w
    z = zxbcdt[..., :D_INNER]
    xbc = zxbcdt[..., D_INNER:D_INNER + CONV_DIM]
    dt_raw = zxbcdt[..., D_INNER + CONV_DIM:]
    xbc = jax.nn.silu(_causal_depthwise_conv(xbc, conv_w, conv_b))
    gn = SSD_GROUPS * SSD_STATE
    xs = xbc[..., :D_INNER].reshape(bsz, nc, SSD_CHUNK, SSD_GROUPS, SSD_HPG, SSD_HEADDIM)
    bm = xbc[..., D_INNER:D_INNER + gn].reshape(bsz, nc, SSD_CHUNK, SSD_GROUPS, SSD_STATE)
    cm = xbc[..., D_INNER + gn:].reshape(bsz, nc, SSD_CHUNK, SSD_GROUPS, SSD_STATE)
    dt = jax.nn.softplus(dt_raw.astype(jnp.float32) + dt_bias.astype(jnp.float32))
    dt = dt.reshape(bsz, nc, SSD_CHUNK, SSD_GROUPS, SSD_HPG)
    a = -jnp.exp(a_log.astype(jnp.float32)).reshape(SSD_GROUPS, SSD_HPG)
    a_cum = jnp.cumsum((dt * a).transpose(0, 3, 4, 1, 2), axis=-1)
    xf = xs.astype(jnp.float32)
    x_dt = xf * dt[..., None]
    causal = jnp.tril(jnp.ones((SSD_CHUNK, SSD_CHUNK), dtype=bool))
    seg = a_cum[..., :, None] - a_cum[..., None, :]
    lmat = jnp.exp(jnp.where(causal, seg, -jnp.inf))
    cb = jnp.einsum('bclgn,bcsgn->bcgls', cm, bm).astype(jnp.float32)
    y_diag = jnp.einsum('bcgls,bghcls,bcsghp->bclghp', cb, lmat, x_dt)
    decay_states = jnp.exp(a_cum[..., -1:] - a_cum)
    states = jnp.einsum('bclgn,bghcl,bclghp->bcghpn', bm.astype(jnp.float32), decay_states, x_dt)
    chunk_decay = jnp.exp(a_cum[..., -1])

    def step(carry, inp):
        st, dec = inp
        return carry * dec[..., None, None] + st, carry

    init = jnp.zeros(states.shape[:1] + states.shape[2:], states.dtype)
    _, prev = lax.scan(step, init, (states.transpose(1, 0, 2, 3, 4, 5), chunk_decay.transpose(3, 0, 1, 2)))
    prev = prev.transpose(1, 0, 2, 3, 4, 5)
    y_off = jnp.einsum('bclgn,bcghpn,bghcl->bclghp', cm.astype(jnp.float32), prev, jnp.exp(a_cum))
    y = y_diag + y_off + xf * d_skip.astype(jnp.float32).reshape(SSD_GROUPS, SSD_HPG, 1)
    y = y.reshape(bsz, seq, D_INNER)
    gated = (y * jax.nn.silu(z.astype(jnp.float32))).reshape(bsz, seq, SSD_GROUPS, D_INNER // SSD_GROUPS)
    gated = gated * lax.rsqrt(jnp.mean(gated * gated, axis=-1, keepdims=True) + NORM_EPS)
    y = gated.reshape(bsz, seq, D_INNER) * norm_g.astype(jnp.float32)
    return y.astype(h.dtype) @ out_w


def _stick_breaking_attention(q, k, v):
    seq = q.shape[1]
    scale = SB_HEADDIM ** -0.5
    outs = []
    for blk in range(seq // SB_QBLOCK):
        q0 = blk * SB_QBLOCK
        kend = q0 + SB_QBLOCK
        qb, kb, vb = q[:, q0:kend], k[:, :kend], v[:, :kend]
        z = jnp.einsum('bqhd,bkhd->bhqk', qb, kb).astype(jnp.float32) * scale
        qpos = q0 + jnp.arange(SB_QBLOCK)
        kpos = jnp.arange(kend)
        mask = kpos[None, :] < qpos[:, None]
        log_1mb = jnp.where(mask, jax.nn.log_sigmoid(-z), 0.0)
        tail = lax.cumsum(log_1mb, axis=3, reverse=True) - log_1mb
        log_a = jnp.where(mask, jax.nn.log_sigmoid(z) + tail, -jnp.inf)
        a = jnp.exp(log_a).astype(vb.dtype)
        outs.append(jnp.einsum('bhqk,bkhd->bqhd', a, vb))
    return jnp.concatenate(outs, axis=1)


def _moe(h, router_w, router_b, w_in, b_in, w_out, b_out):
    bsz, seq, d = h.shape
    t = h.reshape(bsz * seq, d)
    logits = (t @ router_w).astype(jnp.float32) + router_b.astype(jnp.float32)
    top_vals, top_idx = lax.top_k(logits, TOP_K)
    gates = jax.nn.softmax(top_vals, axis=-1)
    combine = jnp.einsum('tk,tke->te', gates, jax.nn.one_hot(top_idx, N_EXPERTS, dtype=jnp.float32))
    out = jnp.zeros((t.shape[0], d), jnp.float32)
    for e in range(N_EXPERTS):
        gu = t @ w_in[e] + b_in[e]
        gate = jnp.minimum(gu[:, 0::2], SWIGLU_LIMIT)
        up = jnp.clip(gu[:, 1::2], -SWIGLU_LIMIT, SWIGLU_LIMIT)
        act = (up + 1.0) * gate * jax.nn.sigmoid(SWIGLU_ALPHA * gate)
        out = out + combine[:, e:e + 1] * (act @ w_out[e] + b_out[e])
    return out.astype(h.dtype).reshape(bsz, seq, d)


def setup_inputs(seed: int = 0) -> dict:
    key = jax.random.key(seed)
    ks = iter(jax.random.split(key, 40))

    def nrm(shape, scale):
        return scale * jax.random.normal(next(ks), shape, jnp.float32)

    def gain(shape):
        return 1.0 + nrm(shape, 0.02)

    dt0 = jnp.exp(jax.random.uniform(next(ks), (N_A_LAYERS, SSD_HEADS), jnp.float32,
                                     minval=np.log(1e-3), maxval=np.log(1e-1)))
    dt_bias = dt0 + jnp.log(-jnp.expm1(-dt0))
    a_log = jnp.log(jax.random.uniform(next(ks), (N_A_LAYERS, SSD_HEADS), jnp.float32, minval=1.0, maxval=16.0))
    return {
        'x': nrm((BATCH, SEQ, D_MODEL), 1.0),
        'c': nrm((BATCH, D_MODEL), 1.0),
        'ada_w': nrm((DEPTH, D_MODEL, 6 * D_MODEL), 0.5 * D_MODEL ** -0.5),
        'ada_b': nrm((DEPTH, 6 * D_MODEL), 0.02),
        'mix_pre_g': gain((DEPTH, D_MODEL)),
        'mix_post_g': gain((DEPTH, D_MODEL)),
        'ffn_pre_g': gain((DEPTH, D_MODEL)),
        'ffn_post_g': gain((DEPTH, D_MODEL)),
        'ssd_in_w': nrm((N_A_LAYERS, D_MODEL, SSD_IN_DIM), D_MODEL ** -0.5),
        'ssd_conv_w': nrm((N_A_LAYERS, SSD_CONV, CONV_DIM), SSD_CONV ** -0.5),
        'ssd_conv_b': nrm((N_A_LAYERS, CONV_DIM), 0.01),
        'ssd_dt_bias': dt_bias,
        'ssd_a_log': a_log,
        'ssd_d': gain((N_A_LAYERS, SSD_HEADS)),
        'ssd_norm_g': gain((N_A_LAYERS, D_INNER)),
        'ssd_out_w': nrm((N_A_LAYERS, D_INNER, D_MODEL), D_INNER ** -0.5),
        'kv_norm_g': gain((D_MODEL,)),
        'kv_w': nrm((D_MODEL, 2 * D_MODEL), D_MODEL ** -0.5),
        'sb_q_w': nrm((N_B_LAYERS, D_MODEL, D_MODEL), D_MODEL ** -0.5),
        'sb_o_w': nrm((N_B_LAYERS, D_MODEL, D_MODEL), D_MODEL ** -0.5),
        'router_w': nrm((DEPTH, D_MODEL, N_EXPERTS), D_MODEL ** -0.5),
        'router_b': nrm((DEPTH, N_EXPERTS), 0.01),
        'exp_w_in': nrm((DEPTH, N_EXPERTS, D_MODEL, 2 * D_FF), D_MODEL ** -0.5),
        'exp_b_in': nrm((DEPTH, N_EXPERTS, 2 * D_FF), 0.01),
        'exp_w_out': nrm((DEPTH, N_EXPERTS, D_FF, D_MODEL), D_FF ** -0.5),
        'exp_b_out': nrm((DEPTH, N_EXPERTS, D_MODEL), 0.01),
    }


def reference(x, c, ada_w, ada_b, mix_pre_g, mix_post_g, ffn_pre_g, ffn_post_g,
              ssd_in_w, ssd_conv_w, ssd_conv_b, ssd_dt_bias, ssd_a_log, ssd_d, ssd_norm_g, ssd_out_w,
              kv_norm_g, kv_w, sb_q_w, sb_o_w,
              router_w, router_b, exp_w_in, exp_b_in, exp_w_out, exp_b_out):
    bsz, seq, _ = x.shape
    c_act = jax.nn.silu(c)
    k_shared = None
    v_shared = None
    for i in range(DEPTH):
        mod = c_act @ ada_w[i] + ada_b[i]
        sh_m, sc_m, g_m, sh_f, sc_f, g_f = jnp.split(mod, 6, axis=-1)
        h = _modulate(x, mix_pre_g[i], sh_m, sc_m)
        if i < N_A_LAYERS:
            j = i
            y = _ssd_mixer(h, ssd_in_w[j], ssd_conv_w[j], ssd_conv_b[j], ssd_dt_bias[j],
                           ssd_a_log[j], ssd_d[j], ssd_norm_g[j], ssd_out_w[j])
        else:
            j = i - N_A_LAYERS
            q = (h @ sb_q_w[j]).reshape(bsz, seq, SB_HEADS, SB_HEADDIM)
            y = _stick_breaking_attention(q, k_shared, v_shared).reshape(bsz, seq, D_MODEL) @ sb_o_w[j]
        x = x + g_m[:, None, :] * _rmsnorm(y, mix_post_g[i])
        h = _modulate(x, ffn_pre_g[i], sh_f, sc_f)
        y = _moe(h, router_w[i], router_b[i], exp_w_in[i], exp_b_in[i], exp_w_out[i], exp_b_out[i])
        x = x + g_f[:, None, :] * _rmsnorm(y, ffn_post_g[i])
        if i == N_A_LAYERS - 1:
            kv = _rmsnorm(x, kv_norm_g) @ kv_w
            k_shared = kv[..., :D_MODEL].reshape(bsz, seq, SB_HEADS, SB_HEADDIM)
            v_shared = kv[..., D_MODEL:].reshape(bsz, seq, SB_HEADS, SB_HEADDIM)
    return x
```

```python
import functools

import jax
import jax.numpy as jnp
from jax import lax
from jax.experimental import pallas as pl
from jax.experimental.pallas import tpu as pltpu

F32 = jnp.float32
BF16 = jnp.bfloat16

D_MODEL = 1024
D_INNER = 2048
SSD_HEADDIM = 64
SSD_HEADS = 32
SSD_GROUPS = 4
SSD_STATE = 128
SSD_CONV = 4
SSD_CHUNK = 128
CONV_DIM = D_INNER + 2 * SSD_GROUPS * SSD_STATE
SB_HEADS = 16
SB_HEADDIM = 64
N_EXPERTS = 32
TOP_K = 4
SWIGLU_LIMIT = 7.0
SWIGLU_ALPHA = 1.702
NORM_EPS = 1e-6

LANES = 128
SUBLANES = 8
ROW_TILES = D_MODEL // LANES
VMEM_LIMIT = 56 * 1024 * 1024

EXP_UNDERFLOW = -88.0


def _cparams(sem):
    return pltpu.CompilerParams(dimension_semantics=sem, vmem_limit_bytes=VMEM_LIMIT)


def _softplus(x):
    return jnp.maximum(x, 0.0) + jnp.log(1.0 + jnp.exp(-jnp.abs(x)))


def _sigmoid(x):
    return 1.0 / (1.0 + jnp.exp(-x))


def _rms(x):
    return x * lax.rsqrt(jnp.mean(x * x, axis=-1, keepdims=True) + NORM_EPS)


def _ada_kernel(c_ref, w_ref, b_ref, o_ref):
    c = c_ref[...]
    ca = (c * _sigmoid(c)).astype(BF16)
    o_ref[...] = jnp.dot(ca, w_ref[...].astype(BF16), preferred_element_type=F32) + b_ref[...]


def _ada_mod(c, ada_w, ada_b):
    depth, d, n = ada_w.shape
    bsz = c.shape[0]
    tn = 1024
    return pl.pallas_call(
        _ada_kernel,
        grid=(depth, n // tn),
        in_specs=[
            pl.BlockSpec((bsz, d), lambda l, j: (0, 0)),
            pl.BlockSpec((None, d, tn), lambda l, j: (l, 0, j)),
            pl.BlockSpec((None, 1, tn), lambda l, j: (l, 0, j)),
        ],
        out_specs=pl.BlockSpec((None, bsz, tn), lambda l, j: (l, 0, j)),
        out_shape=jax.ShapeDtypeStruct((depth, bsz, n), F32),
        compiler_params=_cparams(("parallel", "parallel")),
        name="ada_mod",
    )(c, ada_w, ada_b.reshape(depth, 1, n))


def _norm_mm_kernel(x_ref, g_ref, sc_ref, sh_ref, *refs, n_w, col_chunk):
    w_refs, o_refs = refs[:n_w], refs[n_w:]
    h = _rms(x_ref[...]) * g_ref[...]
    h = h * (1.0 + sc_ref[...]) + sh_ref[...]
    hb = h.astype(BF16)
    for w_ref, o_ref in zip(w_refs, o_refs):
        n = w_ref.shape[1]
        for c0 in range(0, n, col_chunk):
            cw = min(col_chunk, n - c0)
            o_ref[:, c0:c0 + cw] = jnp.dot(
                hb, w_ref[:, c0:c0 + cw], preferred_element_type=F32).astype(o_ref.dtype)


def _norm_mm(x, g, scale, shift, weights, out_dtypes, seq, name):
    t, d = x.shape
    bsz = scale.shape[0]
    tm = 256
    n_w = len(weights)
    in_specs = [
        pl.BlockSpec((tm, d), lambda i: (i, 0)),
        pl.BlockSpec((1, d), lambda i: (0, 0)),
        pl.BlockSpec((None, 1, d), lambda i: (i * tm // seq, 0, 0)),
        pl.BlockSpec((None, 1, d), lambda i: (i * tm // seq, 0, 0)),
    ] + [pl.BlockSpec(w.shape, lambda i: (0, 0)) for w in weights]
    out_specs = [pl.BlockSpec((tm, w.shape[1]), lambda i: (i, 0)) for w in weights]
    out_shape = [jax.ShapeDtypeStruct((t, w.shape[1]), dt) for w, dt in zip(weights, out_dtypes)]
    return pl.pallas_call(
        functools.partial(_norm_mm_kernel, n_w=n_w, col_chunk=512),
        grid=(t // tm,),
        in_specs=in_specs,
        out_specs=out_specs,
        out_shape=out_shape,
        compiler_params=_cparams(("parallel",)),
        name=name,
    )(x, g.reshape(1, d), scale.reshape(bsz, 1, d), shift.reshape(bsz, 1, d), *weights)


def _mm_resid_kernel(y_ref, w_ref, x_ref, gate_ref, pg_ref, o_ref):
    y = jnp.dot(y_ref[...], w_ref[...], preferred_element_type=F32)
    o_ref[...] = x_ref[...] + gate_ref[...] * (_rms(y) * pg_ref[...])


def _mm_resid(y, w, x, gate, post_g, seq, name):
    t, k = y.shape
    d = w.shape[1]
    bsz = gate.shape[0]
    tm = 512
    return pl.pallas_call(
        _mm_resid_kernel,
        grid=(t // tm,),
        in_specs=[
            pl.BlockSpec((tm, k), lambda i: (i, 0)),
            pl.BlockSpec((k, d), lambda i: (0, 0)),
            pl.BlockSpec((tm, d), lambda i: (i, 0)),
            pl.BlockSpec((None, 1, d), lambda i: (i * tm // seq, 0, 0)),
            pl.BlockSpec((1, d), lambda i: (0, 0)),
        ],
        out_specs=pl.BlockSpec((tm, d), lambda i: (i, 0)),
        out_shape=jax.ShapeDtypeStruct((t, d), F32),
        compiler_params=_cparams(("parallel",)),
        name=name,
    )(y, w, x, gate.reshape(bsz, 1, d), post_g.reshape(1, d))


def _ssd_kernel(z_ref, xbc_ref, dt_ref, cw_ref, cb_ref, dtb_ref, alog_ref, dsk_ref, ng_ref, e_ref,
                y_ref, ext_ref, state_ref, act_ref, yacc_ref):
    L = SSD_CHUNK
    c = pl.program_id(1)

    @pl.when(c == 0)
    def _():
        ext_ref[0:SUBLANES, :] = jnp.zeros((SUBLANES, CONV_DIM), F32)
        state_ref[...] = jnp.zeros(state_ref.shape, F32)

    ext_ref[SUBLANES:SUBLANES + L, :] = xbc_ref[...].astype(F32)
    cc = 512
    for c0 in range(0, CONV_DIM, cc):
        acc = cb_ref[:, c0:c0 + cc] + ext_ref[SUBLANES:SUBLANES + L, c0:c0 + cc] * cw_ref[3:4, c0:c0 + cc]
        for j in range(SSD_CONV - 1):
            r0 = SUBLANES - (SSD_CONV - 1) + j
            acc = acc + ext_ref[r0:r0 + L, c0:c0 + cc] * cw_ref[j:j + 1, c0:c0 + cc]
        act_ref[:, c0:c0 + cc] = acc * _sigmoid(acc)
    ext_ref[0:SUBLANES, :] = ext_ref[L:L + SUBLANES, :]

    row = lax.broadcasted_iota(jnp.int32, (L, L), 0)
    col = lax.broadcasted_iota(jnp.int32, (L, L), 1)
    causal = col <= row
    lane_lo = col < SSD_HEADDIM

    dt = _softplus(dt_ref[...] + dtb_ref[...])
    da = dt * (-jnp.exp(alog_ref[...]))
    ltri = jnp.where(causal, 1.0, 0.0).astype(BF16)
    da_hi = da.astype(BF16)
    da_lo = (da - da_hi.astype(F32)).astype(BF16)
    a_cum = (jnp.dot(ltri, da_hi, preferred_element_type=F32)
             + jnp.dot(ltri, da_lo, preferred_element_type=F32))
    a_cum_t = a_cum.T
    exp_a = jnp.exp(a_cum)
    decay = jnp.exp(a_cum[L - 1:L, :] - a_cum)

    e = e_ref[...]
    dt_x = jnp.dot(dt.astype(BF16), e, preferred_element_type=F32)
    exp_a_x = jnp.dot(exp_a.astype(BF16), e, preferred_element_type=F32)
    decay_x = jnp.dot(decay.astype(BF16), e, preferred_element_type=F32)

    gn = SSD_GROUPS * SSD_STATE
    pairs_per_group = SSD_HEADS // SSD_GROUPS // 2
    for g in range(SSD_GROUPS):
        b_g = act_ref[:, D_INNER + g * SSD_STATE:D_INNER + (g + 1) * SSD_STATE]
        c_g = act_ref[:, D_INNER + gn + g * SSD_STATE:D_INNER + gn + (g + 1) * SSD_STATE].astype(BF16)
        b_gt = b_g.T.astype(BF16)
        cb = jnp.dot(c_g, b_gt, preferred_element_type=F32)
        for pp in range(pairs_per_group):
            p = g * pairs_per_group + pp
            sl = slice(p * LANES, (p + 1) * LANES)
            xs_p = act_ref[:, sl]
            xdt = xs_p * dt_x[:, sl]
            xdt_b = xdt.astype(BF16)
            w_b = (xdt * decay_x[:, sl]).astype(BF16)
            yd = []
            for i in range(2):
                h = 2 * p + i
                seg = a_cum[:, h:h + 1] - a_cum_t[h:h + 1, :]
                lm = jnp.where(causal, jnp.exp(jnp.minimum(seg, 0.0)), 0.0)
                m = (cb * lm).astype(BF16)
                yd.append(jnp.dot(m, xdt_b, preferred_element_type=F32))
            y_diag = jnp.where(lane_lo, yd[0], yd[1])
            prev_t = state_ref[p]
            y_off = jnp.dot(c_g, prev_t.astype(BF16), preferred_element_type=F32) * exp_a_x[:, sl]
            s_t = jnp.dot(b_gt, w_b, preferred_element_type=F32)
            state_ref[p] = prev_t * exp_a_x[L - 1:L, sl] + s_t
            yacc_ref[:, sl] = y_diag + y_off + xs_p * dsk_ref[:, sl]

    gw = D_INNER // SSD_GROUPS
    for g in range(SSD_GROUPS):
        sl = slice(g * gw, (g + 1) * gw)
        zf = z_ref[:, sl].astype(F32)
        gated = yacc_ref[:, sl] * (zf * _sigmoid(zf))
        y_ref[:, sl] = (_rms(gated) * ng_ref[:, sl]).astype(y_ref.dtype)


def _ssd(z, xbc, dt_raw, conv_w, conv_b, dt_bias, a_log, d_skip, norm_g, bsz, seq):
    t = z.shape[0]
    nc = seq // SSD_CHUNK
    L = SSD_CHUNK
    pad = LANES - SSD_HEADS
    dtb = jnp.pad(dt_bias, (0, pad)).reshape(1, LANES)
    alog = jnp.pad(a_log, (0, pad)).reshape(1, LANES)
    dsk = jnp.repeat(d_skip, SSD_HEADDIM).reshape(1, D_INNER)
    expand = (jnp.arange(LANES)[:, None] == (jnp.arange(D_INNER)[None, :] // SSD_HEADDIM)).astype(BF16)
    full = lambda shape: pl.BlockSpec(shape, lambda b, c: (0, 0))
    return pl.pallas_call(
        _ssd_kernel,
        grid=(bsz, nc),
        in_specs=[
            pl.BlockSpec((L, D_INNER), lambda b, c: (b * nc + c, 0)),
            pl.BlockSpec((L, CONV_DIM), lambda b, c: (b * nc + c, 0)),
            pl.BlockSpec((L, LANES), lambda b, c: (b * nc + c, 0)),
            full((SSD_CONV, CONV_DIM)),
            full((1, CONV_DIM)),
            full((1, LANES)),
            full((1, LANES)),
            full((1, D_INNER)),
            full((1, D_INNER)),
            full((LANES, D_INNER)),
        ],
        out_specs=pl.BlockSpec((L, D_INNER), lambda b, c: (b * nc + c, 0)),
        out_shape=jax.ShapeDtypeStruct((t, D_INNER), BF16),
        scratch_shapes=[
            pltpu.VMEM((L + 2 * SUBLANES, CONV_DIM), F32),
            pltpu.VMEM((SSD_HEADS // 2, SSD_STATE, LANES), F32),
            pltpu.VMEM((L, CONV_DIM), F32),
            pltpu.VMEM((L, D_INNER), F32),
        ],
        compiler_params=_cparams(("arbitrary", "arbitrary")),
        name="ssd_scan",
    )(z, xbc, dt_raw, conv_w, conv_b.reshape(1, CONV_DIM), dtb, alog, dsk,
      norm_g.reshape(1, D_INNER), expand)


def _attn_kernel(q_ref, k_ref, v_ref, o_ref, carry_ref, acc_ref):
    bq = q_ref.shape[0]
    qi = pl.program_id(2)
    scale = SB_HEADDIM ** -0.5
    row = lax.broadcasted_iota(jnp.int32, (bq, bq), 0)
    col = lax.broadcasted_iota(jnp.int32, (bq, bq), 1)
    strict = col < row
    upper = jnp.where(row > col, 1.0, 0.0).astype(BF16)
    lane_lo = lax.broadcasted_iota(jnp.int32, (bq, LANES), 1) < SB_HEADDIM
    q = q_ref[...]
    for i in range(2):
        qm = jnp.where(lane_lo if i == 0 else jnp.logical_not(lane_lo), q, jnp.zeros_like(q))
        carry_ref[...] = jnp.zeros(carry_ref.shape, F32)
        acc_ref[i] = jnp.zeros((bq, LANES), F32)

        def cond(st):
            kb, go = st
            return jnp.logical_and(kb >= 0, go > 0)

        def body(st):
            kb, _ = st
            k0 = pl.multiple_of(kb * bq, bq)
            kblk = k_ref[pl.ds(k0, bq), :]
            vblk = v_ref[pl.ds(k0, bq), :]
            z = lax.dot_general(qm, kblk, (((1,), (1,)), ((), ())),
                                preferred_element_type=F32) * scale
            sp = _softplus(z)
            mask = jnp.logical_or(strict, kb < qi)
            l1m = jnp.where(mask, -sp, 0.0)
            tail = jnp.dot(l1m.astype(BF16), upper, preferred_element_type=F32)
            cr = carry_ref[...]
            log_a = (z - sp) + tail + cr[:, 0:1]
            a = jnp.where(mask, jnp.exp(log_a), 0.0)
            acc_ref[i] += jnp.dot(a.astype(BF16), vblk, preferred_element_type=F32)
            cnew = cr + jnp.sum(l1m, axis=1, keepdims=True)
            carry_ref[...] = cnew
            go = (jnp.max(cnew) > EXP_UNDERFLOW).astype(jnp.int32)
            return kb - 1, go

        lax.while_loop(cond, body, (qi, jnp.int32(1)))
    o_ref[...] = jnp.where(lane_lo, acc_ref[0], acc_ref[1]).astype(o_ref.dtype)


def _attention(q, kv, bsz, seq):
    t = q.shape[0]
    bq = 128
    nq = seq // bq
    npair = SB_HEADS // 2
    return pl.pallas_call(
        _attn_kernel,
        grid=(bsz, npair, nq),
        in_specs=[
            pl.BlockSpec((bq, LANES), lambda b, p, i: (b * nq + i, p)),
            pl.BlockSpec((seq, LANES), lambda b, p, i: (b, p)),
            pl.BlockSpec((seq, LANES), lambda b, p, i: (b, npair + p)),
        ],
        out_specs=pl.BlockSpec((bq, LANES), lambda b, p, i: (b * nq + i, p)),
        out_shape=jax.ShapeDtypeStruct((t, D_MODEL), BF16),
        scratch_shapes=[
            pltpu.VMEM((bq, LANES), F32),
            pltpu.VMEM((2, bq, LANES), F32),
        ],
        compiler_params=_cparams(("parallel", "parallel", "arbitrary")),
        name="sb_attention",
    )(q, kv, kv)


def _router_kernel(x_ref, g_ref, sc_ref, sh_ref, rw_ref, rb_ref,
                   h_ref, idx_ref, gate_ref, rank_ref, cnt_ref, run_ref):
    tm = x_ref.shape[0]
    i = pl.program_id(0)

    @pl.when(i == 0)
    def _():
        run_ref[...] = jnp.zeros(run_ref.shape, F32)

    h = _rms(x_ref[...]) * g_ref[...]
    h = h * (1.0 + sc_ref[...]) + sh_ref[...]
    for s in range(ROW_TILES):
        h_ref[pl.ds(s, tm, stride=ROW_TILES), :] = h[:, s * LANES:(s + 1) * LANES]

    w = rw_ref[...]
    h_hi = h.astype(BF16)
    h_lo = (h - h_hi.astype(F32)).astype(BF16)
    w_hi = w.astype(BF16)
    w_lo = (w - w_hi.astype(F32)).astype(BF16)
    logits = (jnp.dot(h_hi, w_hi, preferred_element_type=F32)
              + jnp.dot(h_hi, w_lo, preferred_element_type=F32)
              + jnp.dot(h_lo, w_hi, preferred_element_type=F32)) + rb_ref[...]

    lane = lax.broadcasted_iota(jnp.int32, (tm, LANES), 1).astype(F32)
    work = logits
    vals, idxs = [], []
    chosen = jnp.zeros((tm, LANES), F32)
    for _ in range(TOP_K):
        m = jnp.max(work, axis=1, keepdims=True)
        am = jnp.min(jnp.where(work == m, lane, float(LANES)), axis=1, keepdims=True)
        hit = lane == am
        vals.append(m)
        idxs.append(am)
        chosen = jnp.where(hit, 1.0, chosen)
        work = jnp.where(hit, -jnp.inf, work)
    es = [jnp.exp(v - vals[0]) for v in vals]
    denom = es[0] + es[1] + es[2] + es[3]

    r = lax.broadcasted_iota(jnp.int32, (tm, tm), 0)
    cidx = lax.broadcasted_iota(jnp.int32, (tm, tm), 1)
    lstrict = jnp.where(cidx < r, 1.0, 0.0).astype(BF16)
    before = jnp.dot(lstrict, chosen.astype(BF16), preferred_element_type=F32) + run_ref[0:1, :]

    idx_out = jnp.zeros((tm, LANES), jnp.int32)
    gate_out = jnp.zeros((tm, LANES), F32)
    rank_out = jnp.zeros((tm, LANES), jnp.int32)
    for k in range(TOP_K):
        rk = jnp.sum(jnp.where(lane == idxs[k], before, 0.0), axis=1, keepdims=True)
        idx_out = jnp.where(lane == k, idxs[k].astype(jnp.int32), idx_out)
        gate_out = jnp.where(lane == k, es[k] / denom, gate_out)
        rank_out = jnp.where(lane == k, rk.astype(jnp.int32), rank_out)
    idx_ref[...] = idx_out
    gate_ref[...] = gate_out
    rank_ref[...] = rank_out
    total = run_ref[0:1, :] + jnp.sum(chosen, axis=0, keepdims=True)
    run_ref[...] = jnp.broadcast_to(total, run_ref.shape)
    cnt_ref[...] = jnp.broadcast_to(total, cnt_ref.shape).astype(jnp.int32)


def _router(x, g, scale, shift, router_w, router_b, seq):
    t, d = x.shape
    bsz = scale.shape[0]
    tm = 512
    pad = LANES - N_EXPERTS
    rw = jnp.pad(router_w, ((0, 0), (0, pad)))
    rb = jnp.pad(router_b, (0, pad), constant_values=-1e30).reshape(1, LANES)
    return pl.pallas_call(
        _router_kernel,
        grid=(t // tm,),
        in_specs=[
            pl.BlockSpec((tm, d), lambda i: (i, 0)),
            pl.BlockSpec((1, d), lambda i: (0, 0)),
            pl.BlockSpec((None, 1, d), lambda i: (i * tm // seq, 0, 0)),
            pl.BlockSpec((None, 1, d), lambda i: (i * tm // seq, 0, 0)),
            pl.BlockSpec((d, LANES), lambda i: (0, 0)),
            pl.BlockSpec((1, LANES), lambda i: (0, 0)),
        ],
        out_specs=[
            pl.BlockSpec((tm * ROW_TILES, LANES), lambda i: (i, 0)),
            pl.BlockSpec((tm, LANES), lambda i: (i, 0)),
            pl.BlockSpec((tm, LANES), lambda i: (i, 0)),
            pl.BlockSpec((tm, LANES), lambda i: (i, 0)),
            pl.BlockSpec((SUBLANES, LANES), lambda i: (0, 0)),
        ],
        out_shape=[
            jax.ShapeDtypeStruct((t * ROW_TILES, LANES), F32),
            jax.ShapeDtypeStruct((t, LANES), jnp.int32),
            jax.ShapeDtypeStruct((t, LANES), F32),
            jax.ShapeDtypeStruct((t, LANES), jnp.int32),
            jax.ShapeDtypeStruct((SUBLANES, LANES), jnp.int32),
        ],
        scratch_shapes=[pltpu.VMEM((SUBLANES, LANES), F32)],
        compiler_params=_cparams(("arbitrary",)),
        name="moe_router",
    )(x, g.reshape(1, d), scale.reshape(bsz, 1, d), shift.reshape(bsz, 1, d), rw, rb)


def _row_copy(src, src_row, dst, dst_row, sem):
    return pltpu.make_async_copy(
        src.at[pl.ds(pl.multiple_of(src_row * ROW_TILES, ROW_TILES), ROW_TILES)],
        dst.at[pl.ds(pl.multiple_of(dst_row * ROW_TILES, ROW_TILES), ROW_TILES)],
        sem)


def _dispatch_kernel(pos_ref, h_ref, xs_ref, sem):
    tm = h_ref.shape[0] // ROW_TILES

    def issue(r, carry):
        for k in range(TOP_K):
            _row_copy(h_ref, r, xs_ref, pos_ref[r * TOP_K + k], sem).start()
        return carry

    lax.fori_loop(0, tm, issue, 0)
    for k in range(TOP_K):
        pltpu.make_async_copy(h_ref, xs_ref.at[pl.ds(0, tm * ROW_TILES)], sem).wait()


def _dispatch(h_tiles, pos_flat):
    rows = h_tiles.shape[0]
    t = rows // ROW_TILES
    tm = 512
    return pl.pallas_call(
        _dispatch_kernel,
        grid=(t // tm,),
        in_specs=[
            pl.BlockSpec((tm * TOP_K,), lambda i: (i,), memory_space=pltpu.SMEM),
            pl.BlockSpec((tm * ROW_TILES, LANES), lambda i: (i, 0)),
        ],
        out_specs=pl.BlockSpec(memory_space=pl.ANY),
        out_shape=jax.ShapeDtypeStruct((rows * TOP_K, LANES), F32),
        scratch_shapes=[pltpu.SemaphoreType.DMA(())],
        compiler_params=_cparams(("arbitrary",)),
        name="moe_dispatch",
    )(pos_flat, h_tiles)


def _ffn_kernel(tile_ref, exp_ref, lo_ref, hi_ref, first_ref,
                xs_ref, wg_ref, wu_ref, bg_ref, bu_ref, wo_ref, bo_ref, ys_ref):
    del tile_ref, exp_ref
    i = pl.program_id(0)
    tm = xs_ref.shape[0] // ROW_TILES
    lo, hi = lo_ref[i], hi_ref[i]

    def compute():
        x = jnp.concatenate(
            [xs_ref[pl.ds(s, tm, stride=ROW_TILES), :] for s in range(ROW_TILES)], axis=1).astype(BF16)
        gate = jnp.dot(x, wg_ref[...], preferred_element_type=F32) + bg_ref[...]
        up = jnp.dot(x, wu_ref[...], preferred_element_type=F32) + bu_ref[...]
        gate = jnp.minimum(gate, SWIGLU_LIMIT)
        up = jnp.clip(up, -SWIGLU_LIMIT, SWIGLU_LIMIT)
        act = (up + 1.0) * gate * _sigmoid(SWIGLU_ALPHA * gate)
        y = jnp.dot(act.astype(BF16), wo_ref[...], preferred_element_type=F32) + bo_ref[...]
        r = lax.broadcasted_iota(jnp.int32, (tm, 1), 0)
        return jnp.where(jnp.logical_and(r >= lo, r < hi), y, 0.0)

    @pl.when(first_ref[i] == 1)
    def _():
        y = compute()
        for s in range(ROW_TILES):
            ys_ref[pl.ds(s, tm, stride=ROW_TILES), :] = y[:, s * LANES:(s + 1) * LANES]

    @pl.when(jnp.logical_and(first_ref[i] == 0, hi > lo))
    def _():
        y = compute()
        for s in range(ROW_TILES):
            ys_ref[pl.ds(s, tm, stride=ROW_TILES), :] += y[:, s * LANES:(s + 1) * LANES]


def _ffn(xs_tiles, items, wg, wu, bg, bu, wo, bo, tm):
    rows = xs_tiles.shape[0]
    n_items = items[0].shape[0]
    d, f = wg.shape[1], wg.shape[2]
    wspec = lambda shape: pl.BlockSpec(shape, lambda i, tile, ex, lo, hi, fi: (ex[i], 0, 0))
    grid_spec = pltpu.PrefetchScalarGridSpec(
        num_scalar_prefetch=5,
        grid=(n_items,),
        in_specs=[
            pl.BlockSpec((tm * ROW_TILES, LANES), lambda i, tile, ex, lo, hi, fi: (tile[i], 0)),
            wspec((None, d, f)),
            wspec((None, d, f)),
            wspec((None, 1, f)),
            wspec((None, 1, f)),
            wspec((None, f, d)),
            wspec((None, 1, d)),
        ],
        out_specs=pl.BlockSpec((tm * ROW_TILES, LANES), lambda i, tile, ex, lo, hi, fi: (tile[i], 0)),
    )
    return pl.pallas_call(
        _ffn_kernel,
        grid_spec=grid_spec,
        out_shape=jax.ShapeDtypeStruct((rows, LANES), F32),
        compiler_params=_cparams(("arbitrary",)),
        name="moe_ffn",
    )(*items, xs_tiles, wg, wu, bg, bu, wo, bo)


def _combine_kernel(pos_ref, gates_ref, x_ref, gf_ref, pg_ref, ys_ref, o_ref, buf_ref, sem):
    tm = x_ref.shape[0]

    def issue(r, carry):
        for k in range(TOP_K):
            _row_copy(ys_ref, pos_ref[r * TOP_K + k], buf_ref, k * tm + r, sem).start()
        return carry

    lax.fori_loop(0, tm, issue, 0)
    for k in range(TOP_K):
        pltpu.make_async_copy(ys_ref.at[pl.ds(0, tm * ROW_TILES)],
                              buf_ref.at[pl.ds(0, tm * ROW_TILES)], sem).wait()

    gates = gates_ref[...]
    moe = None
    for k in range(TOP_K):
        yk = jnp.concatenate(
            [buf_ref[pl.ds(k * tm * ROW_TILES + s, tm, stride=ROW_TILES), :] for s in range(ROW_TILES)],
            axis=1)
        term = gates[:, k:k + 1] * yk
        moe = term if moe is None else moe + term
    o_ref[...] = x_ref[...] + gf_ref[...] * (_rms(moe) * pg_ref[...])


def _combine(ys_tiles, pos_flat, gates, x, gate_f, post_g, seq):
    t, d = x.shape
    bsz = gate_f.shape[0]
    tm = 256
    return pl.pallas_call(
        _combine_kernel,
        grid=(t // tm,),
        in_specs=[
            pl.BlockSpec((tm * TOP_K,), lambda i: (i,), memory_space=pltpu.SMEM),
            pl.BlockSpec((tm, LANES), lambda i: (i, 0)),
            pl.BlockSpec((tm, d), lambda i: (i, 0)),
            pl.BlockSpec((None, 1, d), lambda i: (i * tm // seq, 0, 0)),
            pl.BlockSpec((1, d), lambda i: (0, 0)),
            pl.BlockSpec(memory_space=pl.ANY),
        ],
        out_specs=pl.BlockSpec((tm, d), lambda i: (i, 0)),
        out_shape=jax.ShapeDtypeStruct((t, d), F32),
        scratch_shapes=[
            pltpu.VMEM((TOP_K * tm * ROW_TILES, LANES), F32),
            pltpu.SemaphoreType.DMA(()),
        ],
        compiler_params=_cparams(("arbitrary",)),
        name="moe_combine",
    )(pos_flat, gates, x, gate_f.reshape(bsz, 1, d), post_g.reshape(1, d), ys_tiles)


def _ffn_items(counts, n_rows, tm):
    n_tiles = n_rows // tm
    n_items = n_tiles + N_EXPERTS - 1
    off = jnp.concatenate([jnp.zeros((1,), jnp.int32), jnp.cumsum(counts).astype(jnp.int32)])
    first_tile = off[:-1] // tm
    last_tile = (off[1:] - 1) // tm
    per = jnp.where(counts > 0, last_tile - first_tile + 1, 0)
    istart = jnp.concatenate([jnp.zeros((1,), jnp.int32), jnp.cumsum(per).astype(jnp.int32)])
    total = istart[-1]
    i = jnp.arange(n_items, dtype=jnp.int32)
    e = jnp.clip(jnp.searchsorted(istart, i, side="right").astype(jnp.int32) - 1, 0, N_EXPERTS - 1)
    valid = i < total
    tile = jnp.where(valid, first_tile[e] + (i - istart[e]), n_tiles - 1)
    lo = jnp.where(valid, jnp.clip(off[e] - tile * tm, 0, tm), 0)
    hi = jnp.where(valid, jnp.clip(off[e + 1] - tile * tm, 0, tm), 0)
    prev_tile = jnp.concatenate([jnp.full((1,), -1, jnp.int32), tile[:-1]])
    first = (tile != prev_tile).astype(jnp.int32)
    return off, (tile.astype(jnp.int32), e, lo.astype(jnp.int32), hi.astype(jnp.int32), first)


def _moe_layer(x, pre_g, scale, shift, gate_f, post_g, router_w, router_b,
               w_in, b_in, w_out, b_out, seq):
    t = x.shape[0]
    ffn_tm = 256
    h_tiles, idx, gates, rank, cnt = _router(x, pre_g, scale, shift, router_w, router_b, seq)
    counts = cnt[0, :N_EXPERTS]
    off, items = _ffn_items(counts, t * TOP_K, ffn_tm)
    pos = (off[idx[:, :TOP_K]] + rank[:, :TOP_K]).astype(jnp.int32).reshape(t * TOP_K)
    xs = _dispatch(h_tiles, pos)
    wg = w_in[:, :, 0::2].astype(BF16)
    wu = w_in[:, :, 1::2].astype(BF16)
    bg = b_in[:, 0::2].reshape(N_EXPERTS, 1, -1)
    bu = b_in[:, 1::2].reshape(N_EXPERTS, 1, -1)
    ys = _ffn(xs, items, wg, wu, bg, bu, w_out.astype(BF16),
              b_out.reshape(N_EXPERTS, 1, -1), ffn_tm)
    return _combine(ys, pos, gates, x, gate_f, post_g, seq)


def kernel(x, c, ada_w, ada_b, mix_pre_g, mix_post_g, ffn_pre_g, ffn_post_g, ssd_in_w, ssd_conv_w,
           ssd_conv_b, ssd_dt_bias, ssd_a_log, ssd_d, ssd_norm_g, ssd_out_w, kv_norm_g, kv_w, sb_q_w,
           sb_o_w, router_w, router_b, exp_w_in, exp_b_in, exp_w_out, exp_b_out):
    bsz, seq, d = x.shape
    t = bsz * seq
    xf = x.reshape(t, d)
    mod = _ada_mod(c, ada_w, ada_b)
    zeros_bd = jnp.zeros((bsz, d), F32)

    sh_m, sc_m, g_m, sh_f, sc_f, g_f = [mod[0, :, k * d:(k + 1) * d] for k in range(6)]
    in_w = ssd_in_w[0]
    w_z = in_w[:, :D_INNER].astype(BF16)
    w_xbc = in_w[:, D_INNER:D_INNER + CONV_DIM].astype(BF16)
    w_dt = jnp.pad(in_w[:, D_INNER + CONV_DIM:], ((0, 0), (0, LANES - SSD_HEADS))).astype(BF16)
    z, xbc, dt_raw = _norm_mm(xf, mix_pre_g[0], sc_m, sh_m, [w_z, w_xbc, w_dt],
                              [BF16, BF16, F32], seq, "ssd_in_proj")
    y = _ssd(z, xbc, dt_raw, ssd_conv_w[0], ssd_conv_b[0], ssd_dt_bias[0], ssd_a_log[0],
             ssd_d[0], ssd_norm_g[0], bsz, seq)
    xf = _mm_resid(y, ssd_out_w[0].astype(BF16), xf, g_m, mix_post_g[0], seq, "ssd_out_proj")
    xf = _moe_layer(xf, ffn_pre_g[0], sc_f, sh_f, g_f, ffn_post_g[0], router_w[0], router_b[0],
                    exp_w_in[0], exp_b_in[0], exp_w_out[0], exp_b_out[0], seq)

    (kv,) = _norm_mm(xf, kv_norm_g, zeros_bd, zeros_bd, [kv_w.astype(BF16)], [BF16], seq, "kv_proj")

    sh_m, sc_m, g_m, sh_f, sc_f, g_f = [mod[1, :, k * d:(k + 1) * d] for k in range(6)]
    (q,) = _norm_mm(xf, mix_pre_g[1], sc_m, sh_m, [sb_q_w[0].astype(BF16)], [BF16], seq, "q_proj")
    att = _attention(q, kv, bsz, seq)
    xf = _mm_resid(att, sb_o_w[0].astype(BF16), xf, g_m, mix_post_g[1], seq, "attn_out_proj")
    xf = _moe_layer(xf, ffn_pre_g[1], sc_f, sh_f, g_f, ffn_post_g[1], router_w[1], router_b[1],
                    exp_w_in[1], exp_b_in[1], exp_w_out[1], exp_b_out[1], seq)
    return xf.reshape(bsz, seq, d)
```

```python
import functools

import jax
import jax.numpy as jnp
from jax import lax
from jax.experimental import pallas as pl
from jax.experimental.pallas import tpu as pltpu

F32 = jnp.float32
BF16 = jnp.bfloat16

D_MODEL = 1024
D_INNER = 2048
SSD_HEADDIM = 64
SSD_HEADS = 32
SSD_GROUPS = 4
SSD_STATE = 128
SSD_CONV = 4
SSD_CHUNK = 128
CONV_DIM = D_INNER + 2 * SSD_GROUPS * SSD_STATE
SB_HEADS = 16
SB_HEADDIM = 64
N_EXPERTS = 32
TOP_K = 4
SWIGLU_LIMIT = 7.0
SWIGLU_ALPHA = 1.702
NORM_EPS = 1e-6

LANES = 128
SUBLANES = 8
ROW_TILES = D_MODEL // LANES
VMEM_LIMIT = 56 * 1024 * 1024

EXP_UNDERFLOW = -88.0


def _cparams(sem):
    return pltpu.CompilerParams(dimension_semantics=sem, vmem_limit_bytes=VMEM_LIMIT)


def _softplus(x):
    return jnp.maximum(x, 0.0) + jnp.log(1.0 + jnp.exp(-jnp.abs(x)))


def _sigmoid(x):
    return 1.0 / (1.0 + jnp.exp(-x))


def _rms(x):
    return x * lax.rsqrt(jnp.mean(x * x, axis=-1, keepdims=True) + NORM_EPS)


def _ada_kernel(c_ref, w_ref, b_ref, o_ref):
    c = c_ref[...]
    ca = (c * _sigmoid(c)).astype(BF16)
    o_ref[...] = jnp.dot(ca, w_ref[...].astype(BF16), preferred_element_type=F32) + b_ref[...]


def _ada_mod(c, ada_w, ada_b):
    depth, d, n = ada_w.shape
    bsz = c.shape[0]
    tn = 1024
    return pl.pallas_call(
        _ada_kernel,
        grid=(depth, n // tn),
        in_specs=[
            pl.BlockSpec((bsz, d), lambda l, j: (0, 0)),
            pl.BlockSpec((None, d, tn), lambda l, j: (l, 0, j)),
            pl.BlockSpec((None, 1, tn), lambda l, j: (l, 0, j)),
        ],
        out_specs=pl.BlockSpec((None, bsz, tn), lambda l, j: (l, 0, j)),
        out_shape=jax.ShapeDtypeStruct((depth, bsz, n), F32),
        compiler_params=_cparams(("parallel", "parallel")),
        name="ada_mod",
    )(c, ada_w, ada_b.reshape(depth, 1, n))


def _norm_mm_kernel(x_ref, g_ref, sc_ref, sh_ref, *refs, n_w, col_chunk):
    w_refs, o_refs = refs[:n_w], refs[n_w:]
    h = _rms(x_ref[...]) * g_ref[...]
    h = h * (1.0 + sc_ref[...]) + sh_ref[...]
    hb = h.astype(BF16)
    for w_ref, o_ref in zip(w_refs, o_refs):
        n = w_ref.shape[1]
        for c0 in range(0, n, col_chunk):
            cw = min(col_chunk, n - c0)
            o_ref[:, c0:c0 + cw] = jnp.dot(
                hb, w_ref[:, c0:c0 + cw], preferred_element_type=F32).astype(o_ref.dtype)


def _norm_mm(x, g, scale, shift, weights, out_dtypes, seq, name):
    t, d = x.shape
    bsz = scale.shape[0]
    tm = 256
    n_w = len(weights)
    in_specs = [
        pl.BlockSpec((tm, d), lambda i: (i, 0)),
        pl.BlockSpec((1, d), lambda i: (0, 0)),
        pl.BlockSpec((None, 1, d), lambda i: (i * tm // seq, 0, 0)),
        pl.BlockSpec((None, 1, d), lambda i: (i * tm // seq, 0, 0)),
    ] + [pl.BlockSpec(w.shape, lambda i: (0, 0)) for w in weights]
    out_specs = [pl.BlockSpec((tm, w.shape[1]), lambda i: (i, 0)) for w in weights]
    out_shape = [jax.ShapeDtypeStruct((t, w.shape[1]), dt) for w, dt in zip(weights, out_dtypes)]
    return pl.pallas_call(
        functools.partial(_norm_mm_kernel, n_w=n_w, col_chunk=512),
        grid=(t // tm,),
        in_specs=in_specs,
        out_specs=out_specs,
        out_shape=out_shape,
        compiler_params=_cparams(("parallel",)),
        name=name,
    )(x, g.reshape(1, d), scale.reshape(bsz, 1, d), shift.reshape(bsz, 1, d), *weights)


def _mm_resid_kernel(y_ref, w_ref, x_ref, gate_ref, pg_ref, o_ref):
    y = jnp.dot(y_ref[...], w_ref[...], preferred_element_type=F32)
    o_ref[...] = x_ref[...] + gate_ref[...] * (_rms(y) * pg_ref[...])


def _mm_resid(y, w, x, gate, post_g, seq, name):
    t, k = y.shape
    d = w.shape[1]
    bsz = gate.shape[0]
    tm = 512
    return pl.pallas_call(
        _mm_resid_kernel,
        grid=(t // tm,),
        in_specs=[
            pl.BlockSpec((tm, k), lambda i: (i, 0)),
            pl.BlockSpec((k, d), lambda i: (0, 0)),
            pl.BlockSpec((tm, d), lambda i: (i, 0)),
            pl.BlockSpec((None, 1, d), lambda i: (i * tm // seq, 0, 0)),
            pl.BlockSpec((1, d), lambda i: (0, 0)),
        ],
        out_specs=pl.BlockSpec((tm, d), lambda i: (i, 0)),
        out_shape=jax.ShapeDtypeStruct((t, d), F32),
        compiler_params=_cparams(("parallel",)),
        name=name,
    )(y, w, x, gate.reshape(bsz, 1, d), post_g.reshape(1, d))


def _ssd_kernel(z_ref, xbc_ref, dt_ref, cw_ref, cb_ref, dtb_ref, alog_ref, dsk_ref, ng_ref, e_ref,
                y_ref, ext_ref, state_ref, act_ref, yacc_ref):
    L = SSD_CHUNK
    c = pl.program_id(1)

    @pl.when(c == 0)
    def _():
        ext_ref[0:SUBLANES, :] = jnp.zeros((SUBLANES, CONV_DIM), F32)
        state_ref[...] = jnp.zeros(state_ref.shape, F32)

    ext_ref[SUBLANES:SUBLANES + L, :] = xbc_ref[...].astype(F32)
    cc = 512
    for c0 in range(0, CONV_DIM, cc):
        acc = cb_ref[:, c0:c0 + cc] + ext_ref[SUBLANES:SUBLANES + L, c0:c0 + cc] * cw_ref[3:4, c0:c0 + cc]
        for j in range(SSD_CONV - 1):
            r0 = SUBLANES - (SSD_CONV - 1) + j
            acc = acc + ext_ref[r0:r0 + L, c0:c0 + cc] * cw_ref[j:j + 1, c0:c0 + cc]
        act_ref[:, c0:c0 + cc] = acc * _sigmoid(acc)
    ext_ref[0:SUBLANES, :] = ext_ref[L:L + SUBLANES, :]

    row = lax.broadcasted_iota(jnp.int32, (L, L), 0)
    col = lax.broadcasted_iota(jnp.int32, (L, L), 1)
    causal = col <= row
    lane_lo = col < SSD_HEADDIM

    dt = _softplus(dt_ref[...] + dtb_ref[...])
    da = dt * (-jnp.exp(alog_ref[...]))
    ltri = jnp.where(causal, 1.0, 0.0).astype(BF16)
    da_hi = da.astype(BF16)
    da_lo = (da - da_hi.astype(F32)).astype(BF16)
    a_cum = (jnp.dot(ltri, da_hi, preferred_element_type=F32)
             + jnp.dot(ltri, da_lo, preferred_element_type=F32))
    a_cum_t = a_cum.T
    exp_a = jnp.exp(a_cum)
    decay = jnp.exp(a_cum[L - 1:L, :] - a_cum)

    e = e_ref[...]
    dt_x = jnp.dot(dt.astype(BF16), e, preferred_element_type=F32)
    exp_a_x = jnp.dot(exp_a.astype(BF16), e, preferred_element_type=F32)
    decay_x = jnp.dot(decay.astype(BF16), e, preferred_element_type=F32)

    gn = SSD_GROUPS * SSD_STATE
    pairs_per_group = SSD_HEADS // SSD_GROUPS // 2
    for g in range(SSD_GROUPS):
        b_g = act_ref[:, D_INNER + g * SSD_STATE:D_INNER + (g + 1) * SSD_STATE]
        c_g = act_ref[:, D_INNER + gn + g * SSD_STATE:D_INNER + gn + (g + 1) * SSD_STATE].astype(BF16)
        b_gt = b_g.T.astype(BF16)
        cb = jnp.dot(c_g, b_gt, preferred_element_type=F32)
        for pp in range(pairs_per_group):
            p = g * pairs_per_group + pp
            sl = slice(p * LANES, (p + 1) * LANES)
            xs_p = act_ref[:, sl]
            xdt = xs_p * dt_x[:, sl]
            xdt_b = xdt.astype(BF16)
            w_b = (xdt * decay_x[:, sl]).astype(BF16)
            yd = []
            for i in range(2):
                h = 2 * p + i
                seg = a_cum[:, h:h + 1] - a_cum_t[h:h + 1, :]
                lm = jnp.where(causal, jnp.exp(jnp.minimum(seg, 0.0)), 0.0)
                m = (cb * lm).astype(BF16)
                yd.append(jnp.dot(m, xdt_b, preferred_element_type=F32))
            y_diag = jnp.where(lane_lo, yd[0], yd[1])
            prev_t = state_ref[p]
            y_off = jnp.dot(c_g, prev_t.astype(BF16), preferred_element_type=F32) * exp_a_x[:, sl]
            s_t = jnp.dot(b_gt, w_b, preferred_element_type=F32)
            state_ref[p] = prev_t * exp_a_x[L - 1:L, sl] + s_t
            yacc_ref[:, sl] = y_diag + y_off + xs_p * dsk_ref[:, sl]

    gw = D_INNER // SSD_GROUPS
    for g in range(SSD_GROUPS):
        sl = slice(g * gw, (g + 1) * gw)
        zf = z_ref[:, sl].astype(F32)
        gated = yacc_ref[:, sl] * (zf * _sigmoid(zf))
        y_ref[:, sl] = (_rms(gated) * ng_ref[:, sl]).astype(y_ref.dtype)


def _ssd(z, xbc, dt_raw, conv_w, conv_b, dt_bias, a_log, d_skip, norm_g, bsz, seq):
    t = z.shape[0]
    nc = seq // SSD_CHUNK
    L = SSD_CHUNK
    pad = LANES - SSD_HEADS
    dtb = jnp.pad(dt_bias, (0, pad)).reshape(1, LANES)
    alog = jnp.pad(a_log, (0, pad)).reshape(1, LANES)
    dsk = jnp.repeat(d_skip, SSD_HEADDIM).reshape(1, D_INNER)
    expand = (jnp.arange(LANES)[:, None] == (jnp.arange(D_INNER)[None, :] // SSD_HEADDIM)).astype(BF16)
    full = lambda shape: pl.BlockSpec(shape, lambda b, c: (0, 0))
    return pl.pallas_call(
        _ssd_kernel,
        grid=(bsz, nc),
        in_specs=[
            pl.BlockSpec((L, D_INNER), lambda b, c: (b * nc + c, 0)),
            pl.BlockSpec((L, CONV_DIM), lambda b, c: (b * nc + c, 0)),
            pl.BlockSpec((L, LANES), lambda b, c: (b * nc + c, 0)),
            full((SSD_CONV, CONV_DIM)),
            full((1, CONV_DIM)),
            full((1, LANES)),
            full((1, LANES)),
            full((1, D_INNER)),
            full((1, D_INNER)),
            full((LANES, D_INNER)),
        ],
        out_specs=pl.BlockSpec((L, D_INNER), lambda b, c: (b * nc + c, 0)),
        out_shape=jax.ShapeDtypeStruct((t, D_INNER), BF16),
        scratch_shapes=[
            pltpu.VMEM((L + 2 * SUBLANES, CONV_DIM), F32),
            pltpu.VMEM((SSD_HEADS // 2, SSD_STATE, LANES), F32),
            pltpu.VMEM((L, CONV_DIM), F32),
            pltpu.VMEM((L, D_INNER), F32),
        ],
        compiler_params=_cparams(("arbitrary", "arbitrary")),
        name="ssd_scan",
    )(z, xbc, dt_raw, conv_w, conv_b.reshape(1, CONV_DIM), dtb, alog, dsk,
      norm_g.reshape(1, D_INNER), expand)


def _attn_kernel(q_ref, k_ref, v_ref, o_ref, carry_ref, acc_ref):
    bq = q_ref.shape[0]
    qi = pl.program_id(2)
    scale = SB_HEADDIM ** -0.5
    row = lax.broadcasted_iota(jnp.int32, (bq, bq), 0)
    col = lax.broadcasted_iota(jnp.int32, (bq, bq), 1)
    strict = col < row
    tail_sum = jnp.concatenate(
        [jnp.where(row > col, 1.0, 0.0), jnp.ones((bq, bq), F32)], axis=1).astype(BF16)
    lane_lo = lax.broadcasted_iota(jnp.int32, (bq, LANES), 1) < SB_HEADDIM
    n_pairs = q_ref.shape[1] // LANES
    heads = [(p, i) for p in range(n_pairs) for i in range(2)]

    carry_ref[...] = jnp.zeros(carry_ref.shape, F32)
    acc_ref[...] = jnp.zeros(acc_ref.shape, F32)

    def block(kb, diagonal):
        k0 = pl.multiple_of(kb * bq, bq)
        sls = [slice(p * LANES, (p + 1) * LANES) for p, _ in heads]
        zs = []
        for (p, i), sl in zip(heads, sls):
            q = q_ref[:, sl]
            qm = jnp.where(lane_lo if i == 0 else jnp.logical_not(lane_lo), q, jnp.zeros_like(q))
            zs.append(lax.dot_general(qm, k_ref[pl.ds(k0, bq), sl], (((1,), (1,)), ((), ())),
                                      preferred_element_type=F32) * scale)
        sps = [_softplus(z) for z in zs]
        l1ms = [(jnp.where(strict, -sp, 0.0) if diagonal else -sp).astype(BF16) for sp in sps]
        tss = [jnp.dot(l1m, tail_sum, preferred_element_type=F32) for l1m in l1ms]
        crs = [carry_ref[n] for n in range(len(heads))]
        a_s = []
        for z, sp, ts, cr in zip(zs, sps, tss, crs):
            a = jnp.exp((z - sp) + ts[:, :bq] + cr)
            if diagonal:
                a = jnp.where(strict, a, 0.0)
            a_s.append(a.astype(BF16))
        pvs = [jnp.dot(a, v_ref[pl.ds(k0, bq), sl], preferred_element_type=F32)
               for a, sl in zip(a_s, sls)]
        cmax = None
        for n in range(len(heads)):
            acc_ref[n] += pvs[n]
            cnew = crs[n] + tss[n][:, bq:]
            carry_ref[n] = cnew
            cmax = cnew if cmax is None else jnp.maximum(cmax, cnew)
        return (jnp.max(cmax) > EXP_UNDERFLOW).astype(jnp.int32)

    go0 = block(qi, True)

    def cond(st):
        kb, go = st
        return jnp.logical_and(kb >= 0, go > 0)

    def body(st):
        kb, _ = st
        return kb - 1, block(kb, False)

    lax.while_loop(cond, body, (qi - 1, go0))
    for p in range(n_pairs):
        o_ref[:, p * LANES:(p + 1) * LANES] = jnp.where(
            lane_lo, acc_ref[2 * p], acc_ref[2 * p + 1]).astype(o_ref.dtype)


def _attention(q, kv, bsz, seq):
    t = q.shape[0]
    bq = 128
    nq = seq // bq
    pairs_per_step = 4
    width = pairs_per_step * LANES
    n_groups = D_MODEL // width
    return pl.pallas_call(
        _attn_kernel,
        grid=(bsz, n_groups, nq),
        in_specs=[
            pl.BlockSpec((bq, width), lambda b, p, i: (b * nq + i, p)),
            pl.BlockSpec((seq, width), lambda b, p, i: (b, p)),
            pl.BlockSpec((seq, width), lambda b, p, i: (b, n_groups + p)),
        ],
        out_specs=pl.BlockSpec((bq, width), lambda b, p, i: (b * nq + i, p)),
        out_shape=jax.ShapeDtypeStruct((t, D_MODEL), BF16),
        scratch_shapes=[
            pltpu.VMEM((2 * pairs_per_step, bq, LANES), F32),
            pltpu.VMEM((2 * pairs_per_step, bq, LANES), F32),
        ],
        compiler_params=_cparams(("parallel", "parallel", "arbitrary")),
        name="sb_attention",
    )(q, kv, kv)


def _router_kernel(x_ref, g_ref, sc_ref, sh_ref, rw_ref, rb_ref,
                   h_ref, idx_ref, gate_ref, rank_ref, cnt_ref, run_ref):
    tm = x_ref.shape[0]
    i = pl.program_id(0)

    @pl.when(i == 0)
    def _():
        run_ref[...] = jnp.zeros(run_ref.shape, F32)

    h = _rms(x_ref[...]) * g_ref[...]
    h = h * (1.0 + sc_ref[...]) + sh_ref[...]
    for s in range(ROW_TILES):
        h_ref[pl.ds(s, tm, stride=ROW_TILES), :] = h[:, s * LANES:(s + 1) * LANES]

    w = rw_ref[...]
    h_hi = h.astype(BF16)
    h_lo = (h - h_hi.astype(F32)).astype(BF16)
    w_hi = w.astype(BF16)
    w_lo = (w - w_hi.astype(F32)).astype(BF16)
    logits = (jnp.dot(h_hi, w_hi, preferred_element_type=F32)
              + jnp.dot(h_hi, w_lo, preferred_element_type=F32)
              + jnp.dot(h_lo, w_hi, preferred_element_type=F32)) + rb_ref[...]

    lane = lax.broadcasted_iota(jnp.int32, (tm, LANES), 1).astype(F32)
    work = logits
    vals, idxs = [], []
    chosen = jnp.zeros((tm, LANES), F32)
    for _ in range(TOP_K):
        m = jnp.max(work, axis=1, keepdims=True)
        am = jnp.min(jnp.where(work == m, lane, float(LANES)), axis=1, keepdims=True)
        hit = lane == am
        vals.append(m)
        idxs.append(am)
        chosen = jnp.where(hit, 1.0, chosen)
        work = jnp.where(hit, -jnp.inf, work)
    es = [jnp.exp(v - vals[0]) for v in vals]
    denom = es[0] + es[1] + es[2] + es[3]

    r = lax.broadcasted_iota(jnp.int32, (tm, tm), 0)
    cidx = lax.broadcasted_iota(jnp.int32, (tm, tm), 1)
    lstrict = jnp.where(cidx < r, 1.0, 0.0).astype(BF16)
    before = jnp.dot(lstrict, chosen.astype(BF16), preferred_element_type=F32) + run_ref[0:1, :]

    idx_out = jnp.zeros((tm, LANES), jnp.int32)
    gate_out = jnp.zeros((tm, LANES), F32)
    rank_out = jnp.zeros((tm, LANES), jnp.int32)
    for k in range(TOP_K):
        rk = jnp.sum(jnp.where(lane == idxs[k], before, 0.0), axis=1, keepdims=True)
        idx_out = jnp.where(lane == k, idxs[k].astype(jnp.int32), idx_out)
        gate_out = jnp.where(lane == k, es[k] / denom, gate_out)
        rank_out = jnp.where(lane == k, rk.astype(jnp.int32), rank_out)
    idx_ref[...] = idx_out
    gate_ref[...] = gate_out
    rank_ref[...] = rank_out
    total = run_ref[0:1, :] + jnp.sum(chosen, axis=0, keepdims=True)
    run_ref[...] = jnp.broadcast_to(total, run_ref.shape)
    cnt_ref[...] = jnp.broadcast_to(total, cnt_ref.shape).astype(jnp.int32)


def _router(x, g, scale, shift, router_w, router_b, seq):
    t, d = x.shape
    bsz = scale.shape[0]
    tm = 512
    pad = LANES - N_EXPERTS
    rw = jnp.pad(router_w, ((0, 0), (0, pad)))
    rb = jnp.pad(router_b, (0, pad), constant_values=-1e30).reshape(1, LANES)
    return pl.pallas_call(
        _router_kernel,
        grid=(t // tm,),
        in_specs=[
            pl.BlockSpec((tm, d), lambda i: (i, 0)),
            pl.BlockSpec((1, d), lambda i: (0, 0)),
            pl.BlockSpec((None, 1, d), lambda i: (i * tm // seq, 0, 0)),
            pl.BlockSpec((None, 1, d), lambda i: (i * tm // seq, 0, 0)),
            pl.BlockSpec((d, LANES), lambda i: (0, 0)),
            pl.BlockSpec((1, LANES), lambda i: (0, 0)),
        ],
        out_specs=[
            pl.BlockSpec((tm * ROW_TILES, LANES), lambda i: (i, 0)),
            pl.BlockSpec((tm, LANES), lambda i: (i, 0)),
            pl.BlockSpec((tm, LANES), lambda i: (i, 0)),
            pl.BlockSpec((tm, LANES), lambda i: (i, 0)),
            pl.BlockSpec((SUBLANES, LANES), lambda i: (0, 0)),
        ],
        out_shape=[
            jax.ShapeDtypeStruct((t * ROW_TILES, LANES), F32),
            jax.ShapeDtypeStruct((t, LANES), jnp.int32),
            jax.ShapeDtypeStruct((t, LANES), F32),
            jax.ShapeDtypeStruct((t, LANES), jnp.int32),
            jax.ShapeDtypeStruct((SUBLANES, LANES), jnp.int32),
        ],
        scratch_shapes=[pltpu.VMEM((SUBLANES, LANES), F32)],
        compiler_params=_cparams(("arbitrary",)),
        name="moe_router",
    )(x, g.reshape(1, d), scale.reshape(bsz, 1, d), shift.reshape(bsz, 1, d), rw, rb)


def _row_copy(src, src_row, dst, dst_row, sem):
    return pltpu.make_async_copy(
        src.at[pl.ds(pl.multiple_of(src_row * ROW_TILES, ROW_TILES), ROW_TILES)],
        dst.at[pl.ds(pl.multiple_of(dst_row * ROW_TILES, ROW_TILES), ROW_TILES)],
        sem)


def _dispatch_kernel(pos_ref, h_ref, xs_ref, sem):
    tm = h_ref.shape[0] // ROW_TILES

    def issue(r, carry):
        for k in range(TOP_K):
            _row_copy(h_ref, r, xs_ref, pos_ref[r * TOP_K + k], sem).start()
        return carry

    lax.fori_loop(0, tm, issue, 0)
    for k in range(TOP_K):
        pltpu.make_async_copy(h_ref, xs_ref.at[pl.ds(0, tm * ROW_TILES)], sem).wait()


def _dispatch(h_tiles, pos_flat):
    rows = h_tiles.shape[0]
    t = rows // ROW_TILES
    tm = 512
    return pl.pallas_call(
        _dispatch_kernel,
        grid=(t // tm,),
        in_specs=[
            pl.BlockSpec((tm * TOP_K,), lambda i: (i,), memory_space=pltpu.SMEM),
            pl.BlockSpec((tm * ROW_TILES, LANES), lambda i: (i, 0)),
        ],
        out_specs=pl.BlockSpec(memory_space=pl.ANY),
        out_shape=jax.ShapeDtypeStruct((rows * TOP_K, LANES), F32),
        scratch_shapes=[pltpu.SemaphoreType.DMA(())],
        compiler_params=_cparams(("arbitrary",)),
        name="moe_dispatch",
    )(pos_flat, h_tiles)


def _ffn_kernel(tile_ref, exp_ref, lo_ref, hi_ref, first_ref, newexp_ref,
                xs_ref, win_ref, bg_ref, bu_ref, wout_ref, bo_ref, perm_ref, ys_ref,
                wg_ref, wu_ref, wo_ref):
    del tile_ref, exp_ref
    i = pl.program_id(0)
    tm = xs_ref.shape[0] // ROW_TILES
    lo, hi = lo_ref[i], hi_ref[i]

    @pl.when(newexp_ref[i] == 1)
    def _():
        pw = 2 * LANES
        for cb in range(win_ref.shape[1] // pw):
            blk = win_ref[:, cb * pw:(cb + 1) * pw].astype(BF16)
            res = jnp.dot(blk, perm_ref[...], preferred_element_type=F32).astype(BF16)
            wg_ref[:, cb * LANES:(cb + 1) * LANES] = res[:, :LANES]
            wu_ref[:, cb * LANES:(cb + 1) * LANES] = res[:, LANES:]
        wo_ref[...] = wout_ref[...].astype(BF16)

    def compute():
        x = jnp.concatenate(
            [xs_ref[pl.ds(s, tm, stride=ROW_TILES), :] for s in range(ROW_TILES)], axis=1).astype(BF16)
        gate = jnp.dot(x, wg_ref[...], preferred_element_type=F32) + bg_ref[...]
        up = jnp.dot(x, wu_ref[...], preferred_element_type=F32) + bu_ref[...]
        gate = jnp.minimum(gate, SWIGLU_LIMIT)
        up = jnp.clip(up, -SWIGLU_LIMIT, SWIGLU_LIMIT)
        act = (up + 1.0) * gate * _sigmoid(SWIGLU_ALPHA * gate)
        y = jnp.dot(act.astype(BF16), wo_ref[...], preferred_element_type=F32) + bo_ref[...]
        r = lax.broadcasted_iota(jnp.int32, (tm, 1), 0)
        return jnp.where(jnp.logical_and(r >= lo, r < hi), y, 0.0)

    @pl.when(first_ref[i] == 1)
    def _():
        y = compute()
        for s in range(ROW_TILES):
            ys_ref[pl.ds(s, tm, stride=ROW_TILES), :] = y[:, s * LANES:(s + 1) * LANES]

    @pl.when(jnp.logical_and(first_ref[i] == 0, hi > lo))
    def _():
        y = compute()
        for s in range(ROW_TILES):
            ys_ref[pl.ds(s, tm, stride=ROW_TILES), :] += y[:, s * LANES:(s + 1) * LANES]


def _ffn(xs_tiles, items, w_in, bg, bu, w_out, bo, tm):
    rows = xs_tiles.shape[0]
    n_items = items[0].shape[0]
    d, f2 = w_in.shape[1], w_in.shape[2]
    f = f2 // 2
    pw = 2 * LANES
    src = jnp.arange(pw)[:, None]
    dst = jnp.arange(pw)[None, :]
    perm = (src == jnp.where(dst < LANES, 2 * dst, 2 * (dst - LANES) + 1)).astype(BF16)
    wspec = lambda shape: pl.BlockSpec(shape, lambda i, tile, ex, lo, hi, fi, ne: (ex[i], 0, 0))
    tspec = pl.BlockSpec((tm * ROW_TILES, LANES), lambda i, tile, ex, lo, hi, fi, ne: (tile[i], 0))
    grid_spec = pltpu.PrefetchScalarGridSpec(
        num_scalar_prefetch=6,
        grid=(n_items,),
        in_specs=[
            tspec,
            wspec((None, d, f2)),
            wspec((None, 1, f)),
            wspec((None, 1, f)),
            wspec((None, f, d)),
            wspec((None, 1, d)),
            pl.BlockSpec((pw, pw), lambda i, tile, ex, lo, hi, fi, ne: (0, 0)),
        ],
        out_specs=tspec,
        scratch_shapes=[
            pltpu.VMEM((d, f), BF16),
            pltpu.VMEM((d, f), BF16),
            pltpu.VMEM((f, d), BF16),
        ],
    )
    return pl.pallas_call(
        _ffn_kernel,
        grid_spec=grid_spec,
        out_shape=jax.ShapeDtypeStruct((rows, LANES), F32),
        compiler_params=_cparams(("arbitrary",)),
        name="moe_ffn",
    )(*items, xs_tiles, w_in, bg, bu, w_out, bo, perm)


def _combine_kernel(pos_ref, gates_ref, x_ref, gf_ref, pg_ref, ys_ref, o_ref, buf_ref, sem):
    tm = x_ref.shape[0]

    def issue(r, carry):
        for k in range(TOP_K):
            _row_copy(ys_ref, pos_ref[r * TOP_K + k], buf_ref, k * tm + r, sem).start()
        return carry

    lax.fori_loop(0, tm, issue, 0)
    for k in range(TOP_K):
        pltpu.make_async_copy(ys_ref.at[pl.ds(0, tm * ROW_TILES)],
                              buf_ref.at[pl.ds(0, tm * ROW_TILES)], sem).wait()

    gates = gates_ref[...]
    moe = None
    for k in range(TOP_K):
        yk = jnp.concatenate(
            [buf_ref[pl.ds(k * tm * ROW_TILES + s, tm, stride=ROW_TILES), :] for s in range(ROW_TILES)],
            axis=1)
        term = gates[:, k:k + 1] * yk
        moe = term if moe is None else moe + term
    o_ref[...] = x_ref[...] + gf_ref[...] * (_rms(moe) * pg_ref[...])


def _combine(ys_tiles, pos_flat, gates, x, gate_f, post_g, seq):
    t, d = x.shape
    bsz = gate_f.shape[0]
    tm = 256
    return pl.pallas_call(
        _combine_kernel,
        grid=(t // tm,),
        in_specs=[
            pl.BlockSpec((tm * TOP_K,), lambda i: (i,), memory_space=pltpu.SMEM),
            pl.BlockSpec((tm, LANES), lambda i: (i, 0)),
            pl.BlockSpec((tm, d), lambda i: (i, 0)),
            pl.BlockSpec((None, 1, d), lambda i: (i * tm // seq, 0, 0)),
            pl.BlockSpec((1, d), lambda i: (0, 0)),
            pl.BlockSpec(memory_space=pl.ANY),
        ],
        out_specs=pl.BlockSpec((tm, d), lambda i: (i, 0)),
        out_shape=jax.ShapeDtypeStruct((t, d), F32),
        scratch_shapes=[
            pltpu.VMEM((TOP_K * tm * ROW_TILES, LANES), F32),
            pltpu.SemaphoreType.DMA(()),
        ],
        compiler_params=_cparams(("arbitrary",)),
        name="moe_combine",
    )(pos_flat, gates, x, gate_f.reshape(bsz, 1, d), post_g.reshape(1, d), ys_tiles)


def _ffn_items(counts, n_rows, tm):
    n_tiles = n_rows // tm
    n_items = n_tiles + N_EXPERTS - 1
    off = jnp.concatenate([jnp.zeros((1,), jnp.int32), jnp.cumsum(counts).astype(jnp.int32)])
    first_tile = off[:-1] // tm
    last_tile = (off[1:] - 1) // tm
    per = jnp.where(counts > 0, last_tile - first_tile + 1, 0)
    istart = jnp.concatenate([jnp.zeros((1,), jnp.int32), jnp.cumsum(per).astype(jnp.int32)])
    total = istart[-1]
    i = jnp.arange(n_items, dtype=jnp.int32)
    e = jnp.sum((istart[None, :] <= i[:, None]).astype(jnp.int32), axis=1) - 1
    e = jnp.clip(e, 0, N_EXPERTS - 1)
    valid = i < total
    e = jnp.where(valid, e, e[jnp.maximum(total - 1, 0)])
    tile = jnp.where(valid, first_tile[e] + (i - istart[e]), n_tiles - 1)
    lo = jnp.where(valid, jnp.clip(off[e] - tile * tm, 0, tm), 0)
    hi = jnp.where(valid, jnp.clip(off[e + 1] - tile * tm, 0, tm), 0)
    prev_tile = jnp.concatenate([jnp.full((1,), -1, jnp.int32), tile[:-1]])
    first = (tile != prev_tile).astype(jnp.int32)
    prev_e = jnp.concatenate([jnp.full((1,), -1, jnp.int32), e[:-1]])
    new_expert = (e != prev_e).astype(jnp.int32)
    return off, (tile.astype(jnp.int32), e.astype(jnp.int32), lo.astype(jnp.int32),
                 hi.astype(jnp.int32), first, new_expert)


def _moe_layer(x, pre_g, scale, shift, gate_f, post_g, router_w, router_b,
               w_in, b_in, w_out, b_out, seq):
    t = x.shape[0]
    ffn_tm = 256
    h_tiles, idx, gates, rank, cnt = _router(x, pre_g, scale, shift, router_w, router_b, seq)
    counts = cnt[0, :N_EXPERTS]
    off, items = _ffn_items(counts, t * TOP_K, ffn_tm)
    pos = (off[idx[:, :TOP_K]] + rank[:, :TOP_K]).astype(jnp.int32).reshape(t * TOP_K)
    xs = _dispatch(h_tiles, pos)
    bg = b_in[:, 0::2].reshape(N_EXPERTS, 1, -1)
    bu = b_in[:, 1::2].reshape(N_EXPERTS, 1, -1)
    ys = _ffn(xs, items, w_in, bg, bu, w_out, b_out.reshape(N_EXPERTS, 1, -1), ffn_tm)
    return _combine(ys, pos, gates, x, gate_f, post_g, seq)


def kernel(x, c, ada_w, ada_b, mix_pre_g, mix_post_g, ffn_pre_g, ffn_post_g, ssd_in_w, ssd_conv_w,
           ssd_conv_b, ssd_dt_bias, ssd_a_log, ssd_d, ssd_norm_g, ssd_out_w, kv_norm_g, kv_w, sb_q_w,
           sb_o_w, router_w, router_b, exp_w_in, exp_b_in, exp_w_out, exp_b_out):
    bsz, seq, d = x.shape
    t = bsz * seq
    xf = x.reshape(t, d)
    mod = _ada_mod(c, ada_w, ada_b)
    zeros_bd = jnp.zeros((bsz, d), F32)

    sh_m, sc_m, g_m, sh_f, sc_f, g_f = [mod[0, :, k * d:(k + 1) * d] for k in range(6)]
    in_w = ssd_in_w[0]
    w_z = in_w[:, :D_INNER].astype(BF16)
    w_xbc = in_w[:, D_INNER:D_INNER + CONV_DIM].astype(BF16)
    w_dt = jnp.pad(in_w[:, D_INNER + CONV_DIM:], ((0, 0), (0, LANES - SSD_HEADS))).astype(BF16)
    z, xbc, dt_raw = _norm_mm(xf, mix_pre_g[0], sc_m, sh_m, [w_z, w_xbc, w_dt],
                              [BF16, BF16, F32], seq, "ssd_in_proj")
    y = _ssd(z, xbc, dt_raw, ssd_conv_w[0], ssd_conv_b[0], ssd_dt_bias[0], ssd_a_log[0],
             ssd_d[0], ssd_norm_g[0], bsz, seq)
    xf = _mm_resid(y, ssd_out_w[0].astype(BF16), xf, g_m, mix_post_g[0], seq, "ssd_out_proj")
    xf = _moe_layer(xf, ffn_pre_g[0], sc_f, sh_f, g_f, ffn_post_g[0], router_w[0], router_b[0],
                    exp_w_in[0], exp_b_in[0], exp_w_out[0], exp_b_out[0], seq)

    (kv,) = _norm_mm(xf, kv_norm_g, zeros_bd, zeros_bd, [kv_w.astype(BF16)], [BF16], seq, "kv_proj")

    sh_m, sc_m, g_m, sh_f, sc_f, g_f = [mod[1, :, k * d:(k + 1) * d] for k in range(6)]
    (q,) = _norm_mm(xf, mix_pre_g[1], sc_m, sh_m, [sb_q_w[0].astype(BF16)], [BF16], seq, "q_proj")
    att = _attention(q, kv, bsz, seq)
    xf = _mm_resid(att, sb_o_w[0].astype(BF16), xf, g_m, mix_post_g[1], seq, "attn_out_proj")
    xf = _moe_layer(xf, ffn_pre_g[1], sc_f, sh_f, g_f, ffn_post_g[1], router_w[1], router_b[1],
                    exp_w_in[1], exp_b_in[1], exp_w_out[1], exp_b_out[1], seq)
    return xf.reshape(bsz, seq, d)
```

```python
import functools

import jax
import jax.numpy as jnp
from jax import lax
from jax.experimental import pallas as pl
from jax.experimental.pallas import tpu as pltpu

F32 = jnp.float32
BF16 = jnp.bfloat16

D_MODEL = 1024
D_INNER = 2048
SSD_HEADDIM = 64
SSD_HEADS = 32
SSD_GROUPS = 4
SSD_STATE = 128
SSD_CONV = 4
SSD_CHUNK = 128
CONV_DIM = D_INNER + 2 * SSD_GROUPS * SSD_STATE
SB_HEADS = 16
SB_HEADDIM = 64
N_EXPERTS = 32
TOP_K = 4
SWIGLU_LIMIT = 7.0
SWIGLU_ALPHA = 1.702
NORM_EPS = 1e-6

LANES = 128
SUBLANES = 8
ROW_TILES = D_MODEL // LANES
VMEM_LIMIT = 56 * 1024 * 1024

EXP_UNDERFLOW = -88.0


def _cparams(sem):
    return pltpu.CompilerParams(dimension_semantics=sem, vmem_limit_bytes=VMEM_LIMIT)


def _softplus(x):
    return jnp.maximum(x, 0.0) + jnp.log(1.0 + jnp.exp(-jnp.abs(x)))


def _sigmoid(x):
    return 1.0 / (1.0 + jnp.exp(-x))


def _rms(x):
    return x * lax.rsqrt(jnp.mean(x * x, axis=-1, keepdims=True) + NORM_EPS)


def _ada_kernel(c_ref, w_ref, b_ref, o_ref):
    c = c_ref[...]
    ca = (c * _sigmoid(c)).astype(BF16)
    o_ref[...] = jnp.dot(ca, w_ref[...].astype(BF16), preferred_element_type=F32) + b_ref[...]


def _ada_mod(c, ada_w, ada_b):
    depth, d, n = ada_w.shape
    bsz = c.shape[0]
    tn = 1024
    return pl.pallas_call(
        _ada_kernel,
        grid=(depth, n // tn),
        in_specs=[
            pl.BlockSpec((bsz, d), lambda l, j: (0, 0)),
            pl.BlockSpec((None, d, tn), lambda l, j: (l, 0, j)),
            pl.BlockSpec((None, 1, tn), lambda l, j: (l, 0, j)),
        ],
        out_specs=pl.BlockSpec((None, bsz, tn), lambda l, j: (l, 0, j)),
        out_shape=jax.ShapeDtypeStruct((depth, bsz, n), F32),
        compiler_params=_cparams(("parallel", "parallel")),
        name="ada_mod",
    )(c, ada_w, ada_b.reshape(depth, 1, n))


def _norm_mm_kernel(x_ref, g_ref, sc_ref, sh_ref, *refs, n_w, col_chunk):
    w_refs, o_refs = refs[:n_w], refs[n_w:]
    h = _rms(x_ref[...]) * g_ref[...]
    h = h * (1.0 + sc_ref[...]) + sh_ref[...]
    hb = h.astype(BF16)
    for w_ref, o_ref in zip(w_refs, o_refs):
        n = w_ref.shape[1]
        for c0 in range(0, n, col_chunk):
            cw = min(col_chunk, n - c0)
            o_ref[:, c0:c0 + cw] = jnp.dot(
                hb, w_ref[:, c0:c0 + cw], preferred_element_type=F32).astype(o_ref.dtype)


def _norm_mm(x, g, scale, shift, weights, out_dtypes, seq, name):
    t, d = x.shape
    bsz = scale.shape[0]
    tm = 256
    n_w = len(weights)
    in_specs = [
        pl.BlockSpec((tm, d), lambda i: (i, 0)),
        pl.BlockSpec((1, d), lambda i: (0, 0)),
        pl.BlockSpec((None, 1, d), lambda i: (i * tm // seq, 0, 0)),
        pl.BlockSpec((None, 1, d), lambda i: (i * tm // seq, 0, 0)),
    ] + [pl.BlockSpec(w.shape, lambda i: (0, 0)) for w in weights]
    out_specs = [pl.BlockSpec((tm, w.shape[1]), lambda i: (i, 0)) for w in weights]
    out_shape = [jax.ShapeDtypeStruct((t, w.shape[1]), dt) for w, dt in zip(weights, out_dtypes)]
    return pl.pallas_call(
        functools.partial(_norm_mm_kernel, n_w=n_w, col_chunk=512),
        grid=(t // tm,),
        in_specs=in_specs,
        out_specs=out_specs,
        out_shape=out_shape,
        compiler_params=_cparams(("parallel",)),
        name=name,
    )(x, g.reshape(1, d), scale.reshape(bsz, 1, d), shift.reshape(bsz, 1, d), *weights)


def _kvq_kernel(x_ref, kvg_ref, g_ref, sc_ref, sh_ref, kvw_ref, qw_ref, kv_ref, q_ref, *, col_chunk):
    r = _rms(x_ref[...])
    hk = (r * kvg_ref[...]).astype(BF16)
    hq = ((r * g_ref[...]) * (1.0 + sc_ref[...]) + sh_ref[...]).astype(BF16)
    for hb, w_ref, o_ref in ((hk, kvw_ref, kv_ref), (hq, qw_ref, q_ref)):
        for c0 in range(0, w_ref.shape[1], col_chunk):
            o_ref[:, c0:c0 + col_chunk] = jnp.dot(
                hb, w_ref[:, c0:c0 + col_chunk], preferred_element_type=F32).astype(o_ref.dtype)


def _kvq_proj(x, kv_g, g, scale, shift, kv_w, q_w, seq):
    t, d = x.shape
    bsz = scale.shape[0]
    tm = 512
    row = lambda i: (i, 0)
    const = lambda i: (0, 0)
    per_batch = lambda i: (i * tm // seq, 0, 0)
    return pl.pallas_call(
        functools.partial(_kvq_kernel, col_chunk=512),
        grid=(t // tm,),
        in_specs=[
            pl.BlockSpec((tm, d), row),
            pl.BlockSpec((1, d), const),
            pl.BlockSpec((1, d), const),
            pl.BlockSpec((None, 1, d), per_batch),
            pl.BlockSpec((None, 1, d), per_batch),
            pl.BlockSpec(kv_w.shape, const),
            pl.BlockSpec(q_w.shape, const),
        ],
        out_specs=[pl.BlockSpec((tm, kv_w.shape[1]), row), pl.BlockSpec((tm, q_w.shape[1]), row)],
        out_shape=[jax.ShapeDtypeStruct((t, kv_w.shape[1]), BF16),
                   jax.ShapeDtypeStruct((t, q_w.shape[1]), BF16)],
        compiler_params=_cparams(("parallel",)),
        name="kv_q_proj",
    )(x, kv_g.reshape(1, d), g.reshape(1, d), scale.reshape(bsz, 1, d), shift.reshape(bsz, 1, d),
      kv_w, q_w)


def _mm_resid_kernel(y_ref, w_ref, x_ref, gate_ref, pg_ref, o_ref):
    y = jnp.dot(y_ref[...], w_ref[...], preferred_element_type=F32)
    o_ref[...] = x_ref[...] + gate_ref[...] * (_rms(y) * pg_ref[...])


def _mm_resid(y, w, x, gate, post_g, seq, name):
    t, k = y.shape
    d = w.shape[1]
    bsz = gate.shape[0]
    tm = 512
    return pl.pallas_call(
        _mm_resid_kernel,
        grid=(t // tm,),
        in_specs=[
            pl.BlockSpec((tm, k), lambda i: (i, 0)),
            pl.BlockSpec((k, d), lambda i: (0, 0)),
            pl.BlockSpec((tm, d), lambda i: (i, 0)),
            pl.BlockSpec((None, 1, d), lambda i: (i * tm // seq, 0, 0)),
            pl.BlockSpec((1, d), lambda i: (0, 0)),
        ],
        out_specs=pl.BlockSpec((tm, d), lambda i: (i, 0)),
        out_shape=jax.ShapeDtypeStruct((t, d), F32),
        compiler_params=_cparams(("parallel",)),
        name=name,
    )(y, w, x, gate.reshape(bsz, 1, d), post_g.reshape(1, d))


def _ssd_kernel(z_ref, xbc_ref, dt_ref, cw_ref, cb_ref, dtb_ref, alog_ref, dsk_ref, ng_ref, e_ref,
                y_ref, ext_ref, state_ref, act_ref, yacc_ref):
    L = SSD_CHUNK
    c = pl.program_id(1)

    @pl.when(c == 0)
    def _():
        ext_ref[0:SUBLANES, :] = jnp.zeros((SUBLANES, CONV_DIM), F32)
        state_ref[...] = jnp.zeros(state_ref.shape, F32)

    ext_ref[SUBLANES:SUBLANES + L, :] = xbc_ref[...].astype(F32)
    cc = 512
    for c0 in range(0, CONV_DIM, cc):
        acc = cb_ref[:, c0:c0 + cc] + ext_ref[SUBLANES:SUBLANES + L, c0:c0 + cc] * cw_ref[3:4, c0:c0 + cc]
        for j in range(SSD_CONV - 1):
            r0 = SUBLANES - (SSD_CONV - 1) + j
            acc = acc + ext_ref[r0:r0 + L, c0:c0 + cc] * cw_ref[j:j + 1, c0:c0 + cc]
        act_ref[:, c0:c0 + cc] = acc * _sigmoid(acc)
    ext_ref[0:SUBLANES, :] = ext_ref[L:L + SUBLANES, :]

    row = lax.broadcasted_iota(jnp.int32, (L, L), 0)
    col = lax.broadcasted_iota(jnp.int32, (L, L), 1)
    causal = col <= row
    lane_lo = col < SSD_HEADDIM

    dt = _softplus(dt_ref[...] + dtb_ref[...])
    da = dt * (-jnp.exp(alog_ref[...]))
    ltri = jnp.where(causal, 1.0, 0.0).astype(BF16)
    da_hi = da.astype(BF16)
    da_lo = (da - da_hi.astype(F32)).astype(BF16)
    a_cum = (jnp.dot(ltri, da_hi, preferred_element_type=F32)
             + jnp.dot(ltri, da_lo, preferred_element_type=F32))
    a_cum_t = a_cum.T
    exp_a = jnp.exp(a_cum)
    decay = jnp.exp(a_cum[L - 1:L, :] - a_cum)

    e = e_ref[...]
    dt_x = jnp.dot(dt.astype(BF16), e, preferred_element_type=F32)
    exp_a_x = jnp.dot(exp_a.astype(BF16), e, preferred_element_type=F32)
    decay_x = jnp.dot(decay.astype(BF16), e, preferred_element_type=F32)

    gn = SSD_GROUPS * SSD_STATE
    pairs_per_group = SSD_HEADS // SSD_GROUPS // 2
    for g in range(SSD_GROUPS):
        b_g = act_ref[:, D_INNER + g * SSD_STATE:D_INNER + (g + 1) * SSD_STATE]
        c_g = act_ref[:, D_INNER + gn + g * SSD_STATE:D_INNER + gn + (g + 1) * SSD_STATE].astype(BF16)
        b_gt = b_g.T.astype(BF16)
        cb = jnp.dot(c_g, b_gt, preferred_element_type=F32)
        for pp in range(pairs_per_group):
            p = g * pairs_per_group + pp
            sl = slice(p * LANES, (p + 1) * LANES)
            xs_p = act_ref[:, sl]
            xdt = xs_p * dt_x[:, sl]
            xdt_b = xdt.astype(BF16)
            w_b = (xdt * decay_x[:, sl]).astype(BF16)
            yd = []
            for i in range(2):
                h = 2 * p + i
                seg = a_cum[:, h:h + 1] - a_cum_t[h:h + 1, :]
                lm = jnp.where(causal, jnp.exp(jnp.minimum(seg, 0.0)), 0.0)
                m = (cb * lm).astype(BF16)
                yd.append(jnp.dot(m, xdt_b, preferred_element_type=F32))
            y_diag = jnp.where(lane_lo, yd[0], yd[1])
            prev_t = state_ref[p]
            y_off = jnp.dot(c_g, prev_t.astype(BF16), preferred_element_type=F32) * exp_a_x[:, sl]
            s_t = jnp.dot(b_gt, w_b, preferred_element_type=F32)
            state_ref[p] = prev_t * exp_a_x[L - 1:L, sl] + s_t
            yacc_ref[:, sl] = y_diag + y_off + xs_p * dsk_ref[:, sl]

    gw = D_INNER // SSD_GROUPS
    for g in range(SSD_GROUPS):
        sl = slice(g * gw, (g + 1) * gw)
        zf = z_ref[:, sl].astype(F32)
        gated = yacc_ref[:, sl] * (zf * _sigmoid(zf))
        y_ref[:, sl] = (_rms(gated) * ng_ref[:, sl]).astype(y_ref.dtype)


def _ssd(z, xbc, dt_raw, conv_w, conv_b, dt_bias, a_log, d_skip, norm_g, bsz, seq):
    t = z.shape[0]
    nc = seq // SSD_CHUNK
    L = SSD_CHUNK
    pad = LANES - SSD_HEADS
    dtb = jnp.pad(dt_bias, (0, pad)).reshape(1, LANES)
    alog = jnp.pad(a_log, (0, pad)).reshape(1, LANES)
    dsk = jnp.repeat(d_skip, SSD_HEADDIM).reshape(1, D_INNER)
    expand = (jnp.arange(LANES)[:, None] == (jnp.arange(D_INNER)[None, :] // SSD_HEADDIM)).astype(BF16)
    full = lambda shape: pl.BlockSpec(shape, lambda b, c: (0, 0))
    return pl.pallas_call(
        _ssd_kernel,
        grid=(bsz, nc),
        in_specs=[
            pl.BlockSpec((L, D_INNER), lambda b, c: (b * nc + c, 0)),
            pl.BlockSpec((L, CONV_DIM), lambda b, c: (b * nc + c, 0)),
            pl.BlockSpec((L, LANES), lambda b, c: (b * nc + c, 0)),
            full((SSD_CONV, CONV_DIM)),
            full((1, CONV_DIM)),
            full((1, LANES)),
            full((1, LANES)),
            full((1, D_INNER)),
            full((1, D_INNER)),
            full((LANES, D_INNER)),
        ],
        out_specs=pl.BlockSpec((L, D_INNER), lambda b, c: (b * nc + c, 0)),
        out_shape=jax.ShapeDtypeStruct((t, D_INNER), BF16),
        scratch_shapes=[
            pltpu.VMEM((L + 2 * SUBLANES, CONV_DIM), F32),
            pltpu.VMEM((SSD_HEADS // 2, SSD_STATE, LANES), F32),
            pltpu.VMEM((L, CONV_DIM), F32),
            pltpu.VMEM((L, D_INNER), F32),
        ],
        compiler_params=_cparams(("arbitrary", "arbitrary")),
        name="ssd_scan",
    )(z, xbc, dt_raw, conv_w, conv_b.reshape(1, CONV_DIM), dtb, alog, dsk,
      norm_g.reshape(1, D_INNER), expand)


def _attn_kernel(q_ref, k_ref, v_ref, o_ref, carry_ref, acc_ref):
    bq = q_ref.shape[0]
    qi = pl.program_id(2)
    scale = SB_HEADDIM ** -0.5
    row = lax.broadcasted_iota(jnp.int32, (bq, bq), 0)
    col = lax.broadcasted_iota(jnp.int32, (bq, bq), 1)
    strict = col < row
    tail_sum = jnp.concatenate(
        [jnp.where(row > col, -1.0, 0.0), jnp.full((bq, bq), -1.0, F32)], axis=1).astype(BF16)
    lane_lo = lax.broadcasted_iota(jnp.int32, (bq, LANES), 1) < SB_HEADDIM
    n_pairs = q_ref.shape[1] // LANES
    heads = [(p, i) for p in range(n_pairs) for i in range(2)]

    carry_ref[...] = jnp.zeros(carry_ref.shape, F32)
    acc_ref[...] = jnp.zeros(acc_ref.shape, F32)

    def block(kb, diagonal):
        k0 = pl.multiple_of(kb * bq, bq)
        sls = [slice(p * LANES, (p + 1) * LANES) for p, _ in heads]
        zs = []
        for (p, i), sl in zip(heads, sls):
            q = q_ref[:, sl] * scale
            qm = jnp.where(lane_lo if i == 0 else jnp.logical_not(lane_lo), q, jnp.zeros_like(q))
            zs.append(lax.dot_general(qm, k_ref[pl.ds(k0, bq), sl], (((1,), (1,)), ((), ())),
                                      preferred_element_type=F32))
        sps = [_softplus(z) for z in zs]
        l1ms = [(jnp.where(strict, sp, 0.0) if diagonal else sp).astype(BF16) for sp in sps]
        tss = [jnp.dot(l1m, tail_sum, preferred_element_type=F32) for l1m in l1ms]
        crs = [carry_ref[n] for n in range(len(heads))]
        a_s = []
        for z, sp, ts, cr in zip(zs, sps, tss, crs):
            a = jnp.exp((z - sp) + ts[:, :bq] + cr)
            if diagonal:
                a = jnp.where(strict, a, 0.0)
            a_s.append(a.astype(BF16))
        pvs = [jnp.dot(a, v_ref[pl.ds(k0, bq), sl], preferred_element_type=F32)
               for a, sl in zip(a_s, sls)]
        cmax = None
        for n in range(len(heads)):
            acc_ref[n] += pvs[n]
            cnew = crs[n] + tss[n][:, bq:]
            carry_ref[n] = cnew
            cmax = cnew if cmax is None else jnp.maximum(cmax, cnew)
        return (jnp.max(cmax) > EXP_UNDERFLOW).astype(jnp.int32)

    go0 = block(qi, True)

    def cond(st):
        kb, go = st
        return jnp.logical_and(kb >= 0, go > 0)

    def body(st):
        kb, _ = st
        return kb - 1, block(kb, False)

    lax.while_loop(cond, body, (qi - 1, go0))
    for p in range(n_pairs):
        o_ref[:, p * LANES:(p + 1) * LANES] = jnp.where(
            lane_lo, acc_ref[2 * p], acc_ref[2 * p + 1]).astype(o_ref.dtype)


def _attention(q, kv, bsz, seq):
    t = q.shape[0]
    bq = 128
    nq = seq // bq
    pairs_per_step = 4
    width = pairs_per_step * LANES
    n_groups = D_MODEL // width
    return pl.pallas_call(
        _attn_kernel,
        grid=(bsz, n_groups, nq),
        in_specs=[
            pl.BlockSpec((bq, width), lambda b, p, i: (b * nq + i, p)),
            pl.BlockSpec((seq, width), lambda b, p, i: (b, p)),
            pl.BlockSpec((seq, width), lambda b, p, i: (b, n_groups + p)),
        ],
        out_specs=pl.BlockSpec((bq, width), lambda b, p, i: (b * nq + i, p)),
        out_shape=jax.ShapeDtypeStruct((t, D_MODEL), BF16),
        scratch_shapes=[
            pltpu.VMEM((2 * pairs_per_step, bq, LANES), F32),
            pltpu.VMEM((2 * pairs_per_step, bq, LANES), F32),
        ],
        compiler_params=_cparams(("parallel", "parallel", "arbitrary")),
        name="sb_attention",
    )(q, kv, kv)


def _router_kernel(x_ref, g_ref, sc_ref, sh_ref, rw_ref, rb_ref,
                   h_ref, idx_ref, gate_ref, rank_ref, cnt_ref, run_ref):
    tm = x_ref.shape[0]
    i = pl.program_id(0)

    @pl.when(i == 0)
    def _():
        run_ref[...] = jnp.zeros(run_ref.shape, F32)

    h = _rms(x_ref[...]) * g_ref[...]
    h = h * (1.0 + sc_ref[...]) + sh_ref[...]
    for s in range(ROW_TILES):
        h_ref[pl.ds(s, tm, stride=ROW_TILES), :] = h[:, s * LANES:(s + 1) * LANES]

    w = rw_ref[...]
    h_hi = h.astype(BF16)
    h_lo = (h - h_hi.astype(F32)).astype(BF16)
    w_hi = w.astype(BF16)
    w_lo = (w - w_hi.astype(F32)).astype(BF16)
    logits = (jnp.dot(h_hi, w_hi, preferred_element_type=F32)
              + jnp.dot(h_hi, w_lo, preferred_element_type=F32)
              + jnp.dot(h_lo, w_hi, preferred_element_type=F32)) + rb_ref[...]

    lane = lax.broadcasted_iota(jnp.int32, (tm, LANES), 1).astype(F32)
    work = logits
    vals, idxs = [], []
    chosen = jnp.zeros((tm, LANES), F32)
    for _ in range(TOP_K):
        m = jnp.max(work, axis=1, keepdims=True)
        am = jnp.min(jnp.where(work == m, lane, float(LANES)), axis=1, keepdims=True)
        hit = lane == am
        vals.append(m)
        idxs.append(am)
        chosen = jnp.where(hit, 1.0, chosen)
        work = jnp.where(hit, -jnp.inf, work)
    es = [jnp.exp(v - vals[0]) for v in vals]
    denom = es[0] + es[1] + es[2] + es[3]

    r = lax.broadcasted_iota(jnp.int32, (tm, tm), 0)
    cidx = lax.broadcasted_iota(jnp.int32, (tm, tm), 1)
    lstrict = jnp.where(cidx < r, 1.0, 0.0).astype(BF16)
    before = jnp.dot(lstrict, chosen.astype(BF16), preferred_element_type=F32) + run_ref[0:1, :]

    idx_out = jnp.zeros((tm, LANES), jnp.int32)
    gate_out = jnp.zeros((tm, LANES), F32)
    rank_out = jnp.zeros((tm, LANES), jnp.int32)
    for k in range(TOP_K):
        rk = jnp.sum(jnp.where(lane == idxs[k], before, 0.0), axis=1, keepdims=True)
        idx_out = jnp.where(lane == k, idxs[k].astype(jnp.int32), idx_out)
        gate_out = jnp.where(lane == k, es[k] / denom, gate_out)
        rank_out = jnp.where(lane == k, rk.astype(jnp.int32), rank_out)
    idx_ref[...] = idx_out
    gate_ref[...] = gate_out
    rank_ref[...] = rank_out
    total = run_ref[0:1, :] + jnp.sum(chosen, axis=0, keepdims=True)
    run_ref[...] = jnp.broadcast_to(total, run_ref.shape)
    cnt_ref[...] = jnp.broadcast_to(total, cnt_ref.shape).astype(jnp.int32)


def _router(x, g, scale, shift, router_w, router_b, seq):
    t, d = x.shape
    bsz = scale.shape[0]
    tm = 512
    pad = LANES - N_EXPERTS
    rw = jnp.pad(router_w, ((0, 0), (0, pad)))
    rb = jnp.pad(router_b, (0, pad), constant_values=-1e30).reshape(1, LANES)
    return pl.pallas_call(
        _router_kernel,
        grid=(t // tm,),
        in_specs=[
            pl.BlockSpec((tm, d), lambda i: (i, 0)),
            pl.BlockSpec((1, d), lambda i: (0, 0)),
            pl.BlockSpec((None, 1, d), lambda i: (i * tm // seq, 0, 0)),
            pl.BlockSpec((None, 1, d), lambda i: (i * tm // seq, 0, 0)),
            pl.BlockSpec((d, LANES), lambda i: (0, 0)),
            pl.BlockSpec((1, LANES), lambda i: (0, 0)),
        ],
        out_specs=[
            pl.BlockSpec((tm * ROW_TILES, LANES), lambda i: (i, 0)),
            pl.BlockSpec((tm, LANES), lambda i: (i, 0)),
            pl.BlockSpec((tm, LANES), lambda i: (i, 0)),
            pl.BlockSpec((tm, LANES), lambda i: (i, 0)),
            pl.BlockSpec((SUBLANES, LANES), lambda i: (0, 0)),
        ],
        out_shape=[
            jax.ShapeDtypeStruct((t * ROW_TILES, LANES), F32),
            jax.ShapeDtypeStruct((t, LANES), jnp.int32),
            jax.ShapeDtypeStruct((t, LANES), F32),
            jax.ShapeDtypeStruct((t, LANES), jnp.int32),
            jax.ShapeDtypeStruct((SUBLANES, LANES), jnp.int32),
        ],
        scratch_shapes=[pltpu.VMEM((SUBLANES, LANES), F32)],
        compiler_params=_cparams(("arbitrary",)),
        name="moe_router",
    )(x, g.reshape(1, d), scale.reshape(bsz, 1, d), shift.reshape(bsz, 1, d), rw, rb)


def _row_copy(src, src_row, dst, dst_row, sem):
    return pltpu.make_async_copy(
        src.at[pl.ds(pl.multiple_of(src_row * ROW_TILES, ROW_TILES), ROW_TILES)],
        dst.at[pl.ds(pl.multiple_of(dst_row * ROW_TILES, ROW_TILES), ROW_TILES)],
        sem)


def _dispatch_kernel(pos_ref, h_ref, xs_ref, sem):
    tm = h_ref.shape[0] // ROW_TILES

    def issue(r, carry):
        for k in range(TOP_K):
            _row_copy(h_ref, r, xs_ref, pos_ref[r * TOP_K + k], sem).start()
        return carry

    lax.fori_loop(0, tm, issue, 0)
    for k in range(TOP_K):
        pltpu.make_async_copy(h_ref, xs_ref.at[pl.ds(0, tm * ROW_TILES)], sem).wait()


def _dispatch(h_tiles, pos_flat):
    rows = h_tiles.shape[0]
    t = rows // ROW_TILES
    tm = 512
    return pl.pallas_call(
        _dispatch_kernel,
        grid=(t // tm,),
        in_specs=[
            pl.BlockSpec((tm * TOP_K,), lambda i: (i,), memory_space=pltpu.SMEM),
            pl.BlockSpec((tm * ROW_TILES, LANES), lambda i: (i, 0)),
        ],
        out_specs=pl.BlockSpec(memory_space=pl.ANY),
        out_shape=jax.ShapeDtypeStruct((rows * TOP_K, LANES), F32),
        scratch_shapes=[pltpu.SemaphoreType.DMA(())],
        compiler_params=_cparams(("arbitrary",)),
        name="moe_dispatch",
    )(pos_flat, h_tiles)


def _ffn_kernel(tile_ref, exp_ref, lo_ref, hi_ref, first_ref, newexp_ref,
                xs_ref, win_ref, bg_ref, bu_ref, wout_ref, bo_ref, perm_ref, ys_ref,
                wg_ref, wu_ref, wo_ref):
    del tile_ref, exp_ref
    i = pl.program_id(0)
    tm = xs_ref.shape[0] // ROW_TILES
    lo, hi = lo_ref[i], hi_ref[i]

    @pl.when(newexp_ref[i] == 1)
    def _():
        pw = 2 * LANES
        for cb in range(win_ref.shape[1] // pw):
            blk = win_ref[:, cb * pw:(cb + 1) * pw].astype(BF16)
            res = jnp.dot(blk, perm_ref[...], preferred_element_type=F32).astype(BF16)
            wg_ref[:, cb * LANES:(cb + 1) * LANES] = res[:, :LANES]
            wu_ref[:, cb * LANES:(cb + 1) * LANES] = res[:, LANES:]
        wo_ref[...] = wout_ref[...].astype(BF16)

    def compute():
        x = jnp.concatenate(
            [xs_ref[pl.ds(s, tm, stride=ROW_TILES), :] for s in range(ROW_TILES)], axis=1).astype(BF16)
        gate = jnp.dot(x, wg_ref[...], preferred_element_type=F32) + bg_ref[...]
        up = jnp.dot(x, wu_ref[...], preferred_element_type=F32) + bu_ref[...]
        gate = jnp.minimum(gate, SWIGLU_LIMIT)
        up = jnp.clip(up, -SWIGLU_LIMIT, SWIGLU_LIMIT)
        act = (up + 1.0) * gate * _sigmoid(SWIGLU_ALPHA * gate)
        y = jnp.dot(act.astype(BF16), wo_ref[...], preferred_element_type=F32) + bo_ref[...]
        r = lax.broadcasted_iota(jnp.int32, (tm, 1), 0)
        return jnp.where(jnp.logical_and(r >= lo, r < hi), y, 0.0)

    @pl.when(first_ref[i] == 1)
    def _():
        y = compute()
        for s in range(ROW_TILES):
            ys_ref[pl.ds(s, tm, stride=ROW_TILES), :] = y[:, s * LANES:(s + 1) * LANES]

    @pl.when(jnp.logical_and(first_ref[i] == 0, hi > lo))
    def _():
        y = compute()
        for s in range(ROW_TILES):
            ys_ref[pl.ds(s, tm, stride=ROW_TILES), :] += y[:, s * LANES:(s + 1) * LANES]


def _ffn(xs_tiles, items, w_in, bg, bu, w_out, bo, layer, tm):
    rows = xs_tiles.shape[0]
    n_items = items[0].shape[0]
    d, f2 = w_in.shape[2], w_in.shape[3]
    f = f2 // 2
    pw = 2 * LANES
    src = jnp.arange(pw)[:, None]
    dst = jnp.arange(pw)[None, :]
    perm = (src == jnp.where(dst < LANES, 2 * dst, 2 * (dst - LANES) + 1)).astype(BF16)
    wspec = lambda shape: pl.BlockSpec(
        shape, lambda i, tile, ex, lo, hi, fi, ne: (layer, ex[i], 0, 0))
    tspec = pl.BlockSpec((tm * ROW_TILES, LANES), lambda i, tile, ex, lo, hi, fi, ne: (tile[i], 0))
    grid_spec = pltpu.PrefetchScalarGridSpec(
        num_scalar_prefetch=6,
        grid=(n_items,),
        in_specs=[
            tspec,
            wspec((None, None, d, f2)),
            wspec((None, None, 1, f)),
            wspec((None, None, 1, f)),
            wspec((None, None, f, d)),
            wspec((None, None, 1, d)),
            pl.BlockSpec((pw, pw), lambda i, tile, ex, lo, hi, fi, ne: (0, 0)),
        ],
        out_specs=tspec,
        scratch_shapes=[
            pltpu.VMEM((d, f), BF16),
            pltpu.VMEM((d, f), BF16),
            pltpu.VMEM((f, d), BF16),
        ],
    )
    return pl.pallas_call(
        _ffn_kernel,
        grid_spec=grid_spec,
        out_shape=jax.ShapeDtypeStruct((rows, LANES), F32),
        compiler_params=_cparams(("arbitrary",)),
        name="moe_ffn",
    )(*items, xs_tiles, w_in, bg, bu, w_out, bo, perm)


def _combine_kernel(pos_ref, pos_next_ref, gates_ref, x_ref, gf_ref, pg_ref, ys_ref, o_ref,
                    buf_ref, sems):
    tm = x_ref.shape[0]
    i = pl.program_id(0)
    slot = i % 2

    def gather(p_ref, dst_slot):
        def issue(r, carry):
            for k in range(TOP_K):
                _row_copy(ys_ref, p_ref[r * TOP_K + k], buf_ref.at[dst_slot], k * tm + r,
                          sems.at[dst_slot]).start()
            return carry
        lax.fori_loop(0, tm, issue, 0)

    @pl.when(i == 0)
    def _():
        gather(pos_ref, 0)

    @pl.when(i + 1 < pl.num_programs(0))
    def _():
        gather(pos_next_ref, 1 - slot)

    for k in range(TOP_K):
        pltpu.make_async_copy(ys_ref.at[pl.ds(0, tm * ROW_TILES)],
                              buf_ref.at[slot, pl.ds(0, tm * ROW_TILES)], sems.at[slot]).wait()

    gates = gates_ref[...]
    moe = None
    for k in range(TOP_K):
        yk = jnp.concatenate(
            [buf_ref[slot, pl.ds(k * tm * ROW_TILES + s, tm, stride=ROW_TILES), :]
             for s in range(ROW_TILES)], axis=1)
        term = gates[:, k:k + 1] * yk
        moe = term if moe is None else moe + term
    o_ref[...] = x_ref[...] + gf_ref[...] * (_rms(moe) * pg_ref[...])


def _combine(ys_tiles, pos_flat, gates, x, gate_f, post_g, seq):
    t, d = x.shape
    bsz = gate_f.shape[0]
    tm = 256
    n_steps = t // tm
    return pl.pallas_call(
        _combine_kernel,
        grid=(n_steps,),
        in_specs=[
            pl.BlockSpec((tm * TOP_K,), lambda i: (i,), memory_space=pltpu.SMEM),
            pl.BlockSpec((tm * TOP_K,), lambda i: (jnp.minimum(i + 1, n_steps - 1),),
                         memory_space=pltpu.SMEM),
            pl.BlockSpec((tm, LANES), lambda i: (i, 0)),
            pl.BlockSpec((tm, d), lambda i: (i, 0)),
            pl.BlockSpec((None, 1, d), lambda i: (i * tm // seq, 0, 0)),
            pl.BlockSpec((1, d), lambda i: (0, 0)),
            pl.BlockSpec(memory_space=pl.ANY),
        ],
        out_specs=pl.BlockSpec((tm, d), lambda i: (i, 0)),
        out_shape=jax.ShapeDtypeStruct((t, d), F32),
        scratch_shapes=[
            pltpu.VMEM((2, TOP_K * tm * ROW_TILES, LANES), F32),
            pltpu.SemaphoreType.DMA((2,)),
        ],
        compiler_params=_cparams(("arbitrary",)),
        name="moe_combine",
    )(pos_flat, pos_flat, gates, x, gate_f.reshape(bsz, 1, d), post_g.reshape(1, d), ys_tiles)


def _ffn_items(counts, n_rows, tm):
    n_tiles = n_rows // tm
    n_items = n_tiles + N_EXPERTS - 1
    off = jnp.concatenate([jnp.zeros((1,), jnp.int32), jnp.cumsum(counts).astype(jnp.int32)])
    first_tile = off[:-1] // tm
    last_tile = (off[1:] - 1) // tm
    per = jnp.where(counts > 0, last_tile - first_tile + 1, 0)
    istart = jnp.concatenate([jnp.zeros((1,), jnp.int32), jnp.cumsum(per).astype(jnp.int32)])
    total = istart[-1]
    i = jnp.arange(n_items, dtype=jnp.int32)
    e = jnp.sum((istart[None, :] <= i[:, None]).astype(jnp.int32), axis=1) - 1
    e = jnp.clip(e, 0, N_EXPERTS - 1)
    valid = i < total
    e = jnp.where(valid, e, e[jnp.maximum(total - 1, 0)])
    tile = jnp.where(valid, first_tile[e] + (i - istart[e]), n_tiles - 1)
    lo = jnp.where(valid, jnp.clip(off[e] - tile * tm, 0, tm), 0)
    hi = jnp.where(valid, jnp.clip(off[e + 1] - tile * tm, 0, tm), 0)
    prev_tile = jnp.concatenate([jnp.full((1,), -1, jnp.int32), tile[:-1]])
    first = (tile != prev_tile).astype(jnp.int32)
    prev_e = jnp.concatenate([jnp.full((1,), -1, jnp.int32), e[:-1]])
    new_expert = (e != prev_e).astype(jnp.int32)
    return off, (tile.astype(jnp.int32), e.astype(jnp.int32), lo.astype(jnp.int32),
                 hi.astype(jnp.int32), first, new_expert)


def _moe_layer(x, pre_g, scale, shift, gate_f, post_g, router_w, router_b,
               w_in, b_in, w_out, b_out, layer, seq):
    t = x.shape[0]
    depth = w_in.shape[0]
    ffn_tm = 512
    h_tiles, idx, gates, rank, cnt = _router(x, pre_g, scale, shift, router_w, router_b, seq)
    counts = cnt[0, :N_EXPERTS]
    off, items = _ffn_items(counts, t * TOP_K, ffn_tm)
    pos = (off[idx[:, :TOP_K]] + rank[:, :TOP_K]).astype(jnp.int32).reshape(t * TOP_K)
    xs = _dispatch(h_tiles, pos)
    bg = b_in[:, :, 0::2].reshape(depth, N_EXPERTS, 1, -1)
    bu = b_in[:, :, 1::2].reshape(depth, N_EXPERTS, 1, -1)
    ys = _ffn(xs, items, w_in, bg, bu, w_out, b_out.reshape(depth, N_EXPERTS, 1, -1), layer, ffn_tm)
    return _combine(ys, pos, gates, x, gate_f, post_g, seq)


def kernel(x, c, ada_w, ada_b, mix_pre_g, mix_post_g, ffn_pre_g, ffn_post_g, ssd_in_w, ssd_conv_w,
           ssd_conv_b, ssd_dt_bias, ssd_a_log, ssd_d, ssd_norm_g, ssd_out_w, kv_norm_g, kv_w, sb_q_w,
           sb_o_w, router_w, router_b, exp_w_in, exp_b_in, exp_w_out, exp_b_out):
    bsz, seq, d = x.shape
    t = bsz * seq
    xf = x.reshape(t, d)
    mod = _ada_mod(c, ada_w, ada_b)

    sh_m, sc_m, g_m, sh_f, sc_f, g_f = [mod[0, :, k * d:(k + 1) * d] for k in range(6)]
    in_w = ssd_in_w[0]
    w_z = in_w[:, :D_INNER].astype(BF16)
    w_xbc = in_w[:, D_INNER:D_INNER + CONV_DIM].astype(BF16)
    w_dt = jnp.pad(in_w[:, D_INNER + CONV_DIM:], ((0, 0), (0, LANES - SSD_HEADS))).astype(BF16)
    z, xbc, dt_raw = _norm_mm(xf, mix_pre_g[0], sc_m, sh_m, [w_z, w_xbc, w_dt],
                              [BF16, BF16, F32], seq, "ssd_in_proj")
    y = _ssd(z, xbc, dt_raw, ssd_conv_w[0], ssd_conv_b[0], ssd_dt_bias[0], ssd_a_log[0],
             ssd_d[0], ssd_norm_g[0], bsz, seq)
    xf = _mm_resid(y, ssd_out_w[0].astype(BF16), xf, g_m, mix_post_g[0], seq, "ssd_out_proj")
    xf = _moe_layer(xf, ffn_pre_g[0], sc_f, sh_f, g_f, ffn_post_g[0], router_w[0], router_b[0],
                    exp_w_in, exp_b_in, exp_w_out, exp_b_out, 0, seq)

    sh_m, sc_m, g_m, sh_f, sc_f, g_f = [mod[1, :, k * d:(k + 1) * d] for k in range(6)]
    kv, q = _kvq_proj(xf, kv_norm_g, mix_pre_g[1], sc_m, sh_m, kv_w.astype(BF16),
                      sb_q_w[0].astype(BF16), seq)
    att = _attention(q, kv, bsz, seq)
    xf = _mm_resid(att, sb_o_w[0].astype(BF16), xf, g_m, mix_post_g[1], seq, "attn_out_proj")
    xf = _moe_layer(xf, ffn_pre_g[1], sc_f, sh_f, g_f, ffn_post_g[1], router_w[1], router_b[1],
                    exp_w_in, exp_b_in, exp_w_out, exp_b_out, 1, seq)
    return xf.reshape(bsz, seq, d)
```

```python
import functools

import jax
import jax.numpy as jnp
from jax import lax
from jax.experimental import pallas as pl
from jax.experimental.pallas import tpu as pltpu

F32 = jnp.float32
BF16 = jnp.bfloat16

D_MODEL = 1024
D_INNER = 2048
SSD_HEADDIM = 64
SSD_HEADS = 32
SSD_GROUPS = 4
SSD_STATE = 128
SSD_CONV = 4
SSD_CHUNK = 128
CONV_DIM = D_INNER + 2 * SSD_GROUPS * SSD_STATE
SB_HEADS = 16
SB_HEADDIM = 64
N_EXPERTS = 32
TOP_K = 4
SWIGLU_LIMIT = 7.0
SWIGLU_ALPHA = 1.702
NORM_EPS = 1e-6

LANES = 128
SUBLANES = 8
ROW_TILES = D_MODEL // LANES
VMEM_LIMIT = 56 * 1024 * 1024

EXP_UNDERFLOW = -88.0

ROUTE_TM = 512


def _cparams(sem):
    return pltpu.CompilerParams(dimension_semantics=sem, vmem_limit_bytes=VMEM_LIMIT)


def _softplus(x):
    return jnp.maximum(x, 0.0) + jnp.log(1.0 + jnp.exp(-jnp.abs(x)))


def _sigmoid(x):
    return 0.5 * jnp.tanh(0.5 * x) + 0.5


def _rms(x):
    return x * lax.rsqrt(jnp.mean(x * x, axis=-1, keepdims=True) + NORM_EPS)


def _ada_kernel(c_ref, w_ref, b_ref, o_ref):
    c = c_ref[...]
    ca = (c * _sigmoid(c)).astype(BF16)
    o_ref[...] = jnp.dot(ca, w_ref[...].astype(BF16), preferred_element_type=F32) + b_ref[...]


def _ada_mod(c, ada_w, ada_b):
    depth, d, n = ada_w.shape
    bsz = c.shape[0]
    tn = 1024
    return pl.pallas_call(
        _ada_kernel,
        grid=(depth, n // tn),
        in_specs=[
            pl.BlockSpec((bsz, d), lambda l, j: (0, 0)),
            pl.BlockSpec((None, d, tn), lambda l, j: (l, 0, j)),
            pl.BlockSpec((None, 1, tn), lambda l, j: (l, 0, j)),
        ],
        out_specs=pl.BlockSpec((None, bsz, tn), lambda l, j: (l, 0, j)),
        out_shape=jax.ShapeDtypeStruct((depth, bsz, n), F32),
        compiler_params=_cparams(("parallel", "parallel")),
        name="ada_mod",
    )(c, ada_w, ada_b.reshape(depth, 1, n))


def _norm_mm_kernel(x_ref, g_ref, sc_ref, sh_ref, *refs, n_w, col_chunk):
    w_refs, o_refs = refs[:n_w], refs[n_w:]
    h = _rms(x_ref[...]) * g_ref[...]
    h = h * (1.0 + sc_ref[...]) + sh_ref[...]
    hb = h.astype(BF16)
    for w_ref, o_ref in zip(w_refs, o_refs):
        n = w_ref.shape[1]
        for c0 in range(0, n, col_chunk):
            cw = min(col_chunk, n - c0)
            o_ref[:, c0:c0 + cw] = jnp.dot(
                hb, w_ref[:, c0:c0 + cw], preferred_element_type=F32).astype(o_ref.dtype)


def _norm_mm(x, g, scale, shift, weights, out_dtypes, seq, name):
    t, d = x.shape
    bsz = scale.shape[0]
    tm = 256
    n_w = len(weights)
    in_specs = [
        pl.BlockSpec((tm, d), lambda i: (i, 0)),
        pl.BlockSpec((1, d), lambda i: (0, 0)),
        pl.BlockSpec((None, 1, d), lambda i: (i * tm // seq, 0, 0)),
        pl.BlockSpec((None, 1, d), lambda i: (i * tm // seq, 0, 0)),
    ] + [pl.BlockSpec(w.shape, lambda i: (0, 0)) for w in weights]
    out_specs = [pl.BlockSpec((tm, w.shape[1]), lambda i: (i, 0)) for w in weights]
    out_shape = [jax.ShapeDtypeStruct((t, w.shape[1]), dt) for w, dt in zip(weights, out_dtypes)]
    return pl.pallas_call(
        functools.partial(_norm_mm_kernel, n_w=n_w, col_chunk=512),
        grid=(t // tm,),
        in_specs=in_specs,
        out_specs=out_specs,
        out_shape=out_shape,
        compiler_params=_cparams(("parallel",)),
        name=name,
    )(x, g.reshape(1, d), scale.reshape(bsz, 1, d), shift.reshape(bsz, 1, d), *weights)


def _kvq_kernel(x_ref, kvg_ref, g_ref, sc_ref, sh_ref, kvw_ref, qw_ref, kv_ref, q_ref, *, col_chunk):
    r = _rms(x_ref[...])
    hk = (r * kvg_ref[...]).astype(BF16)
    hq = ((r * g_ref[...]) * (1.0 + sc_ref[...]) + sh_ref[...]).astype(BF16)
    for hb, w_ref, o_ref in ((hk, kvw_ref, kv_ref), (hq, qw_ref, q_ref)):
        for c0 in range(0, w_ref.shape[1], col_chunk):
            o_ref[:, c0:c0 + col_chunk] = jnp.dot(
                hb, w_ref[:, c0:c0 + col_chunk], preferred_element_type=F32).astype(o_ref.dtype)


def _kvq_proj(x, kv_g, g, scale, shift, kv_w, q_w, seq):
    t, d = x.shape
    bsz = scale.shape[0]
    tm = 512
    row = lambda i: (i, 0)
    const = lambda i: (0, 0)
    per_batch = lambda i: (i * tm // seq, 0, 0)
    return pl.pallas_call(
        functools.partial(_kvq_kernel, col_chunk=512),
        grid=(t // tm,),
        in_specs=[
            pl.BlockSpec((tm, d), row),
            pl.BlockSpec((1, d), const),
            pl.BlockSpec((1, d), const),
            pl.BlockSpec((None, 1, d), per_batch),
            pl.BlockSpec((None, 1, d), per_batch),
            pl.BlockSpec(kv_w.shape, const),
            pl.BlockSpec(q_w.shape, const),
        ],
        out_specs=[pl.BlockSpec((tm, kv_w.shape[1]), row), pl.BlockSpec((tm, q_w.shape[1]), row)],
        out_shape=[jax.ShapeDtypeStruct((t, kv_w.shape[1]), BF16),
                   jax.ShapeDtypeStruct((t, q_w.shape[1]), BF16)],
        compiler_params=_cparams(("parallel",)),
        name="kv_q_proj",
    )(x, kv_g.reshape(1, d), g.reshape(1, d), scale.reshape(bsz, 1, d), shift.reshape(bsz, 1, d),
      kv_w, q_w)


def _mm_resid_kernel(y_ref, w_ref, x_ref, gate_ref, pg_ref, o_ref):
    y = jnp.dot(y_ref[...], w_ref[...], preferred_element_type=F32)
    o_ref[...] = x_ref[...] + gate_ref[...] * (_rms(y) * pg_ref[...])


def _mm_resid(y, w, x, gate, post_g, seq, name):
    t, k = y.shape
    d = w.shape[1]
    bsz = gate.shape[0]
    tm = 512
    return pl.pallas_call(
        _mm_resid_kernel,
        grid=(t // tm,),
        in_specs=[
            pl.BlockSpec((tm, k), lambda i: (i, 0)),
            pl.BlockSpec((k, d), lambda i: (0, 0)),
            pl.BlockSpec((tm, d), lambda i: (i, 0)),
            pl.BlockSpec((None, 1, d), lambda i: (i * tm // seq, 0, 0)),
            pl.BlockSpec((1, d), lambda i: (0, 0)),
        ],
        out_specs=pl.BlockSpec((tm, d), lambda i: (i, 0)),
        out_shape=jax.ShapeDtypeStruct((t, d), F32),
        compiler_params=_cparams(("parallel",)),
        name=name,
    )(y, w, x, gate.reshape(bsz, 1, d), post_g.reshape(1, d))


def _ssd_kernel(z_ref, xbc_ref, dt_ref, cw_ref, cb_ref, dtb_ref, alog_ref, dsk_ref, ng_ref, e_ref,
                y_ref, ext_ref, state_ref, act_ref, yacc_ref):
    L = SSD_CHUNK
    c = pl.program_id(1)

    @pl.when(c == 0)
    def _():
        ext_ref[0:SUBLANES, :] = jnp.zeros((SUBLANES, CONV_DIM), F32)
        state_ref[...] = jnp.zeros(state_ref.shape, F32)

    ext_ref[SUBLANES:SUBLANES + L, :] = xbc_ref[...].astype(F32)
    cc = 512
    for c0 in range(0, CONV_DIM, cc):
        acc = cb_ref[:, c0:c0 + cc] + ext_ref[SUBLANES:SUBLANES + L, c0:c0 + cc] * cw_ref[3:4, c0:c0 + cc]
        for j in range(SSD_CONV - 1):
            r0 = SUBLANES - (SSD_CONV - 1) + j
            acc = acc + ext_ref[r0:r0 + L, c0:c0 + cc] * cw_ref[j:j + 1, c0:c0 + cc]
        act_ref[:, c0:c0 + cc] = acc * _sigmoid(acc)
    ext_ref[0:SUBLANES, :] = ext_ref[L:L + SUBLANES, :]

    row = lax.broadcasted_iota(jnp.int32, (L, L), 0)
    col = lax.broadcasted_iota(jnp.int32, (L, L), 1)
    causal = col <= row
    lane_lo = col < SSD_HEADDIM

    dt = _softplus(dt_ref[...] + dtb_ref[...])
    da = dt * (-jnp.exp(alog_ref[...]))
    ltri = jnp.where(causal, 1.0, 0.0).astype(BF16)
    da_hi = da.astype(BF16)
    da_lo = (da - da_hi.astype(F32)).astype(BF16)
    a_cum = (jnp.dot(ltri, da_hi, preferred_element_type=F32)
             + jnp.dot(ltri, da_lo, preferred_element_type=F32))
    a_cum_t = a_cum.T
    exp_a = jnp.exp(a_cum)
    decay = jnp.exp(a_cum[L - 1:L, :] - a_cum)

    e = e_ref[...]
    dt_x = jnp.dot(dt.astype(BF16), e, preferred_element_type=F32)
    exp_a_x = jnp.dot(exp_a.astype(BF16), e, preferred_element_type=F32)
    decay_x = jnp.dot(decay.astype(BF16), e, preferred_element_type=F32)

    gn = SSD_GROUPS * SSD_STATE
    pairs_per_group = SSD_HEADS // SSD_GROUPS // 2
    for g in range(SSD_GROUPS):
        b_g = act_ref[:, D_INNER + g * SSD_STATE:D_INNER + (g + 1) * SSD_STATE]
        c_g = act_ref[:, D_INNER + gn + g * SSD_STATE:D_INNER + gn + (g + 1) * SSD_STATE].astype(BF16)
        b_gt = b_g.T.astype(BF16)
        cb = jnp.dot(c_g, b_gt, preferred_element_type=F32)
        for pp in range(pairs_per_group):
            p = g * pairs_per_group + pp
            sl = slice(p * LANES, (p + 1) * LANES)
            xs_p = act_ref[:, sl]
            xdt = xs_p * dt_x[:, sl]
            xdt_b = xdt.astype(BF16)
            w_b = (xdt * decay_x[:, sl]).astype(BF16)
            yd = []
            for i in range(2):
                h = 2 * p + i
                seg = a_cum[:, h:h + 1] - a_cum_t[h:h + 1, :]
                lm = jnp.where(causal, jnp.exp(jnp.minimum(seg, 0.0)), 0.0)
                m = (cb * lm).astype(BF16)
                yd.append(jnp.dot(m, xdt_b, preferred_element_type=F32))
            y_diag = jnp.where(lane_lo, yd[0], yd[1])
            prev_t = state_ref[p]
            y_off = jnp.dot(c_g, prev_t.astype(BF16), preferred_element_type=F32) * exp_a_x[:, sl]
            s_t = jnp.dot(b_gt, w_b, preferred_element_type=F32)
            state_ref[p] = prev_t * exp_a_x[L - 1:L, sl] + s_t
            yacc_ref[:, sl] = y_diag + y_off + xs_p * dsk_ref[:, sl]

    gw = D_INNER // SSD_GROUPS
    for g in range(SSD_GROUPS):
        sl = slice(g * gw, (g + 1) * gw)
        zf = z_ref[:, sl].astype(F32)
        gated = yacc_ref[:, sl] * (zf * _sigmoid(zf))
        y_ref[:, sl] = (_rms(gated) * ng_ref[:, sl]).astype(y_ref.dtype)


def _ssd(z, xbc, dt_raw, conv_w, conv_b, dt_bias, a_log, d_skip, norm_g, bsz, seq):
    t = z.shape[0]
    nc = seq // SSD_CHUNK
    L = SSD_CHUNK
    pad = LANES - SSD_HEADS
    dtb = jnp.pad(dt_bias, (0, pad)).reshape(1, LANES)
    alog = jnp.pad(a_log, (0, pad)).reshape(1, LANES)
    dsk = jnp.repeat(d_skip, SSD_HEADDIM).reshape(1, D_INNER)
    expand = (jnp.arange(LANES)[:, None] == (jnp.arange(D_INNER)[None, :] // SSD_HEADDIM)).astype(BF16)
    full = lambda shape: pl.BlockSpec(shape, lambda b, c: (0, 0))
    return pl.pallas_call(
        _ssd_kernel,
        grid=(bsz, nc),
        in_specs=[
            pl.BlockSpec((L, D_INNER), lambda b, c: (b * nc + c, 0)),
            pl.BlockSpec((L, CONV_DIM), lambda b, c: (b * nc + c, 0)),
            pl.BlockSpec((L, LANES), lambda b, c: (b * nc + c, 0)),
            full((SSD_CONV, CONV_DIM)),
            full((1, CONV_DIM)),
            full((1, LANES)),
            full((1, LANES)),
            full((1, D_INNER)),
            full((1, D_INNER)),
            full((LANES, D_INNER)),
        ],
        out_specs=pl.BlockSpec((L, D_INNER), lambda b, c: (b * nc + c, 0)),
        out_shape=jax.ShapeDtypeStruct((t, D_INNER), BF16),
        scratch_shapes=[
            pltpu.VMEM((L + 2 * SUBLANES, CONV_DIM), F32),
            pltpu.VMEM((SSD_HEADS // 2, SSD_STATE, LANES), F32),
            pltpu.VMEM((L, CONV_DIM), F32),
            pltpu.VMEM((L, D_INNER), F32),
        ],
        compiler_params=_cparams(("arbitrary", "arbitrary")),
        name="ssd_scan",
    )(z, xbc, dt_raw, conv_w, conv_b.reshape(1, CONV_DIM), dtb, alog, dsk,
      norm_g.reshape(1, D_INNER), expand)


def _attn_kernel(q_ref, k_ref, v_ref, o_ref, carry_ref, acc_ref):
    bq = q_ref.shape[0]
    qi = pl.program_id(2)
    scale = SB_HEADDIM ** -0.5
    row = lax.broadcasted_iota(jnp.int32, (bq, bq), 0)
    col = lax.broadcasted_iota(jnp.int32, (bq, bq), 1)
    strict = col < row
    tail_sum = jnp.concatenate(
        [jnp.where(row > col, -1.0, 0.0), jnp.full((bq, bq), -1.0, F32)], axis=1).astype(BF16)
    lane_lo = lax.broadcasted_iota(jnp.int32, (bq, LANES), 1) < SB_HEADDIM
    n_pairs = q_ref.shape[1] // LANES
    heads = [(p, i) for p in range(n_pairs) for i in range(2)]

    carry_ref[...] = jnp.zeros(carry_ref.shape, F32)
    acc_ref[...] = jnp.zeros(acc_ref.shape, F32)

    def block(kb, diagonal):
        k0 = pl.multiple_of(kb * bq, bq)
        sls = [slice(p * LANES, (p + 1) * LANES) for p, _ in heads]
        zs = []
        for (p, i), sl in zip(heads, sls):
            q = q_ref[:, sl] * scale
            qm = jnp.where(lane_lo if i == 0 else jnp.logical_not(lane_lo), q, jnp.zeros_like(q))
            zs.append(lax.dot_general(qm, k_ref[pl.ds(k0, bq), sl], (((1,), (1,)), ((), ())),
                                      preferred_element_type=F32))
        sps = [_softplus(z) for z in zs]
        l1ms = [(jnp.where(strict, sp, 0.0) if diagonal else sp).astype(BF16) for sp in sps]
        tss = [jnp.dot(l1m, tail_sum, preferred_element_type=F32) for l1m in l1ms]
        crs = [carry_ref[n] for n in range(len(heads))]
        a_s = []
        for z, sp, ts, cr in zip(zs, sps, tss, crs):
            a = jnp.exp((z - sp) + ts[:, :bq] + cr)
            if diagonal:
                a = jnp.where(strict, a, 0.0)
            a_s.append(a.astype(BF16))
        pvs = [jnp.dot(a, v_ref[pl.ds(k0, bq), sl], preferred_element_type=F32)
               for a, sl in zip(a_s, sls)]
        cmax = None
        for n in range(len(heads)):
            acc_ref[n] += pvs[n]
            cnew = crs[n] + tss[n][:, bq:]
            carry_ref[n] = cnew
            cmax = cnew if cmax is None else jnp.maximum(cmax, cnew)
        return (jnp.max(cmax) > EXP_UNDERFLOW).astype(jnp.int32)

    go0 = block(qi, True)

    def cond(st):
        kb, go = st
        return jnp.logical_and(kb >= 0, go > 0)

    def body(st):
        kb, _ = st
        return kb - 1, block(kb, False)

    lax.while_loop(cond, body, (qi - 1, go0))
    for p in range(n_pairs):
        o_ref[:, p * LANES:(p + 1) * LANES] = jnp.where(
            lane_lo, acc_ref[2 * p], acc_ref[2 * p + 1]).astype(o_ref.dtype)


def _attention(q, kv, bsz, seq):
    t = q.shape[0]
    bq = 128
    nq = seq // bq
    pairs_per_step = 4
    width = pairs_per_step * LANES
    n_groups = D_MODEL // width
    return pl.pallas_call(
        _attn_kernel,
        grid=(bsz, n_groups, nq),
        in_specs=[
            pl.BlockSpec((bq, width), lambda b, p, i: (b * nq + i, p)),
            pl.BlockSpec((seq, width), lambda b, p, i: (b, p)),
            pl.BlockSpec((seq, width), lambda b, p, i: (b, n_groups + p)),
        ],
        out_specs=pl.BlockSpec((bq, width), lambda b, p, i: (b * nq + i, p)),
        out_shape=jax.ShapeDtypeStruct((t, D_MODEL), BF16),
        scratch_shapes=[
            pltpu.VMEM((2 * pairs_per_step, bq, LANES), F32),
            pltpu.VMEM((2 * pairs_per_step, bq, LANES), F32),
        ],
        compiler_params=_cparams(("parallel", "parallel", "arbitrary")),
        name="sb_attention",
    )(q, kv, kv)


def _router_kernel(x_ref, g_ref, sc_ref, sh_ref, rw_ref, rb_ref,
                   h_ref, idx_ref, gate_ref, rank_ref, lpos_ref, tcnt_ref, run_ref):
    tm = x_ref.shape[0]
    i = pl.program_id(0)

    @pl.when(i == 0)
    def _():
        run_ref[...] = jnp.zeros(run_ref.shape, F32)

    h = _rms(x_ref[...]) * g_ref[...]
    h = h * (1.0 + sc_ref[...]) + sh_ref[...]
    for s in range(ROW_TILES):
        h_ref[pl.ds(s, tm, stride=ROW_TILES), :] = h[:, s * LANES:(s + 1) * LANES]

    w = rw_ref[...]
    h_hi = h.astype(BF16)
    h_lo = (h - h_hi.astype(F32)).astype(BF16)
    w_hi = w.astype(BF16)
    w_lo = (w - w_hi.astype(F32)).astype(BF16)
    logits = (jnp.dot(h_hi, w_hi, preferred_element_type=F32)
              + jnp.dot(h_hi, w_lo, preferred_element_type=F32)
              + jnp.dot(h_lo, w_hi, preferred_element_type=F32)) + rb_ref[...]

    lane = lax.broadcasted_iota(jnp.int32, (tm, LANES), 1).astype(F32)
    work = logits
    vals, idxs = [], []
    chosen = jnp.zeros((tm, LANES), F32)
    for _ in range(TOP_K):
        m = jnp.max(work, axis=1, keepdims=True)
        am = jnp.min(jnp.where(work == m, lane, float(LANES)), axis=1, keepdims=True)
        hit = lane == am
        vals.append(m)
        idxs.append(am)
        chosen = jnp.where(hit, 1.0, chosen)
        work = jnp.where(hit, -jnp.inf, work)
    es = [jnp.exp(v - vals[0]) for v in vals]
    denom = es[0] + es[1] + es[2] + es[3]

    r = lax.broadcasted_iota(jnp.int32, (tm, tm), 0)
    cidx = lax.broadcasted_iota(jnp.int32, (tm, tm), 1)
    lstrict = jnp.where(cidx < r, 1.0, 0.0).astype(BF16)
    before_tile = jnp.dot(lstrict, chosen.astype(BF16), preferred_element_type=F32)
    before = before_tile + run_ref[0:1, :]

    tile_cnt = jnp.sum(chosen, axis=0, keepdims=True)
    cnt_hi = jnp.floor(tile_cnt * (1.0 / 16.0))
    cnt_lo = tile_cnt - 16.0 * cnt_hi
    er = lax.broadcasted_iota(jnp.int32, (LANES, LANES), 0)
    ec = lax.broadcasted_iota(jnp.int32, (LANES, LANES), 1)
    below = jnp.where(er < ec, 1.0, 0.0).astype(BF16)
    bc = lambda v: jnp.broadcast_to(v, (SUBLANES, LANES)).astype(BF16)
    seg_start = (16.0 * jnp.dot(bc(cnt_hi), below, preferred_element_type=F32)
                 + jnp.dot(bc(cnt_lo), below, preferred_element_type=F32))[0:1, :]
    local = before_tile + seg_start

    idx_out = jnp.zeros((tm, LANES), jnp.int32)
    gate_out = jnp.zeros((tm, LANES), F32)
    rank_out = jnp.zeros((tm, LANES), jnp.int32)
    lpos_out = jnp.zeros((tm, LANES), jnp.int32)
    for k in range(TOP_K):
        sel = lane == idxs[k]
        rk = jnp.sum(jnp.where(sel, before, 0.0), axis=1, keepdims=True)
        lp = jnp.sum(jnp.where(sel, local, 0.0), axis=1, keepdims=True)
        idx_out = jnp.where(lane == k, idxs[k].astype(jnp.int32), idx_out)
        gate_out = jnp.where(lane == k, es[k] / denom, gate_out)
        rank_out = jnp.where(lane == k, rk.astype(jnp.int32), rank_out)
        lpos_out = jnp.where(lane == k, lp.astype(jnp.int32), lpos_out)
    idx_ref[...] = idx_out
    gate_ref[...] = gate_out
    rank_ref[...] = rank_out
    lpos_ref[...] = lpos_out
    tcnt_ref[...] = jnp.broadcast_to(tile_cnt, tcnt_ref.shape).astype(jnp.int32)
    total = run_ref[0:1, :] + tile_cnt
    run_ref[...] = jnp.broadcast_to(total, run_ref.shape)


def _router(x, g, scale, shift, router_w, router_b, seq):
    t, d = x.shape
    bsz = scale.shape[0]
    tm = ROUTE_TM
    pad = LANES - N_EXPERTS
    rw = jnp.pad(router_w, ((0, 0), (0, pad)))
    rb = jnp.pad(router_b, (0, pad), constant_values=-1e30).reshape(1, LANES)
    return pl.pallas_call(
        _router_kernel,
        grid=(t // tm,),
        in_specs=[
            pl.BlockSpec((tm, d), lambda i: (i, 0)),
            pl.BlockSpec((1, d), lambda i: (0, 0)),
            pl.BlockSpec((None, 1, d), lambda i: (i * tm // seq, 0, 0)),
            pl.BlockSpec((None, 1, d), lambda i: (i * tm // seq, 0, 0)),
            pl.BlockSpec((d, LANES), lambda i: (0, 0)),
            pl.BlockSpec((1, LANES), lambda i: (0, 0)),
        ],
        out_specs=[
            pl.BlockSpec((tm * ROW_TILES, LANES), lambda i: (i, 0)),
            pl.BlockSpec((tm, LANES), lambda i: (i, 0)),
            pl.BlockSpec((tm, LANES), lambda i: (i, 0)),
            pl.BlockSpec((tm, LANES), lambda i: (i, 0)),
            pl.BlockSpec((tm, LANES), lambda i: (i, 0)),
            pl.BlockSpec((SUBLANES, LANES), lambda i: (i, 0)),
        ],
        out_shape=[
            jax.ShapeDtypeStruct((t * ROW_TILES, LANES), F32),
            jax.ShapeDtypeStruct((t, LANES), jnp.int32),
            jax.ShapeDtypeStruct((t, LANES), F32),
            jax.ShapeDtypeStruct((t, LANES), jnp.int32),
            jax.ShapeDtypeStruct((t, LANES), jnp.int32),
            jax.ShapeDtypeStruct((t // tm * SUBLANES, LANES), jnp.int32),
        ],
        scratch_shapes=[pltpu.VMEM((SUBLANES, LANES), F32)],
        compiler_params=_cparams(("arbitrary",)),
        name="moe_router",
    )(x, g.reshape(1, d), scale.reshape(bsz, 1, d), shift.reshape(bsz, 1, d), rw, rb)


def _row_copy(src, src_row, dst, dst_row, sem):
    return pltpu.make_async_copy(
        src.at[pl.ds(pl.multiple_of(src_row * ROW_TILES, ROW_TILES), ROW_TILES)],
        dst.at[pl.ds(pl.multiple_of(dst_row * ROW_TILES, ROW_TILES), ROW_TILES)],
        sem)


def _dispatch_kernel(cnt_ref, start_ref, base_ref, lpos_ref, h_ref, xs_ref, loc_ref, sems):
    tm = h_ref.shape[0] // ROW_TILES
    i = pl.program_id(0)
    slot = i % 2
    slot_rows = tm * TOP_K * ROW_TILES
    unroll = 4

    def wait_slot(s):
        pltpu.make_async_copy(loc_ref.at[s], xs_ref.at[pl.ds(0, slot_rows)], sems.at[s]).wait()

    def place(j, carry):
        for u in range(unroll):
            t = j * unroll + u
            v = h_ref[pl.ds(pl.multiple_of(t * ROW_TILES, ROW_TILES), ROW_TILES), :]
            for k in range(TOP_K):
                p = lpos_ref[t * TOP_K + k]
                loc_ref[slot, pl.ds(pl.multiple_of(p * ROW_TILES, ROW_TILES), ROW_TILES), :] = v
        return carry

    lax.fori_loop(0, tm // unroll, place, 0)

    top_bit = tm.bit_length() - 1

    def segment(e, carry):
        n = cnt_ref[i * N_EXPERTS + e]
        src = start_ref[i * N_EXPERTS + e]
        dst = base_ref[i * N_EXPERTS + e]
        for b in range(top_bit, -1, -1):
            done = (n >> (b + 1)) << (b + 1)
            size = (1 << b) * ROW_TILES

            @pl.when(((n >> b) & 1) == 1)
            def _():
                pltpu.make_async_copy(
                    loc_ref.at[slot, pl.ds(pl.multiple_of((src + done) * ROW_TILES, ROW_TILES), size)],
                    xs_ref.at[pl.ds(pl.multiple_of((dst + done) * ROW_TILES, ROW_TILES), size)],
                    sems.at[slot]).start()
        return carry

    lax.fori_loop(0, N_EXPERTS, segment, 0)

    @pl.when(i > 0)
    def _():
        wait_slot(1 - slot)

    @pl.when(i == pl.num_programs(0) - 1)
    def _():
        wait_slot(slot)


def _dispatch(h_tiles, lpos_flat, tile_cnt, off):
    rows = h_tiles.shape[0]
    t = rows // ROW_TILES
    tm = ROUTE_TM
    n_tiles = t // tm
    zeros_col = jnp.zeros((n_tiles, 1), jnp.int32)
    seg_start = jnp.concatenate([zeros_col, jnp.cumsum(tile_cnt, axis=1)[:, :-1]], axis=1)
    before = jnp.concatenate(
        [jnp.zeros((1, N_EXPERTS), jnp.int32), jnp.cumsum(tile_cnt, axis=0)[:-1]], axis=0)
    base = off[None, :N_EXPERTS] + before
    flat = lambda a: a.astype(jnp.int32).reshape(n_tiles * N_EXPERTS)
    grid_spec = pltpu.PrefetchScalarGridSpec(
        num_scalar_prefetch=3,
        grid=(n_tiles,),
        in_specs=[
            pl.BlockSpec((tm * TOP_K,), lambda i, c, s, b: (i,), memory_space=pltpu.SMEM),
            pl.BlockSpec((tm * ROW_TILES, LANES), lambda i, c, s, b: (i, 0)),
        ],
        out_specs=pl.BlockSpec(memory_space=pl.ANY),
        scratch_shapes=[
            pltpu.VMEM((2, tm * TOP_K * ROW_TILES, LANES), F32),
            pltpu.SemaphoreType.DMA((2,)),
        ],
    )
    return pl.pallas_call(
        _dispatch_kernel,
        grid_spec=grid_spec,
        out_shape=jax.ShapeDtypeStruct((rows * TOP_K, LANES), F32),
        compiler_params=_cparams(("arbitrary",)),
        name="moe_dispatch",
    )(flat(tile_cnt), flat(seg_start), flat(base), lpos_flat, h_tiles)


def _ffn_kernel(tile_ref, exp_ref, lo_ref, hi_ref, first_ref, newexp_ref,
                xs_ref, win_ref, bg_ref, bu_ref, wout_ref, bo_ref, perm_ref, ys_ref,
                wg_ref, wu_ref, wo_ref):
    del tile_ref, exp_ref
    i = pl.program_id(0)
    tm = xs_ref.shape[0] // ROW_TILES
    lo, hi = lo_ref[i], hi_ref[i]

    @pl.when(newexp_ref[i] == 1)
    def _():
        pw = 2 * LANES
        for cb in range(win_ref.shape[1] // pw):
            blk = win_ref[:, cb * pw:(cb + 1) * pw].astype(BF16)
            res = jnp.dot(blk, perm_ref[...], preferred_element_type=F32).astype(BF16)
            wg_ref[:, cb * LANES:(cb + 1) * LANES] = res[:, :LANES]
            wu_ref[:, cb * LANES:(cb + 1) * LANES] = res[:, LANES:]
        wo_ref[...] = wout_ref[...].astype(BF16)

    def compute():
        x = jnp.concatenate(
            [xs_ref[pl.ds(s, tm, stride=ROW_TILES), :] for s in range(ROW_TILES)], axis=1).astype(BF16)
        gate = jnp.dot(x, wg_ref[...], preferred_element_type=F32) + bg_ref[...]
        up = jnp.dot(x, wu_ref[...], preferred_element_type=F32) + bu_ref[...]
        gate = jnp.minimum(gate, SWIGLU_LIMIT)
        up = jnp.clip(up, -SWIGLU_LIMIT, SWIGLU_LIMIT)
        act = (up + 1.0) * gate * _sigmoid(SWIGLU_ALPHA * gate)
        y = jnp.dot(act.astype(BF16), wo_ref[...], preferred_element_type=F32) + bo_ref[...]
        r = lax.broadcasted_iota(jnp.int32, (tm, 1), 0)
        return jnp.where(jnp.logical_and(r >= lo, r < hi), y, 0.0)

    @pl.when(first_ref[i] == 1)
    def _():
        y = compute()
        for s in range(ROW_TILES):
            ys_ref[pl.ds(s, tm, stride=ROW_TILES), :] = y[:, s * LANES:(s + 1) * LANES]

    @pl.when(jnp.logical_and(first_ref[i] == 0, hi > lo))
    def _():
        y = compute()
        for s in range(ROW_TILES):
            ys_ref[pl.ds(s, tm, stride=ROW_TILES), :] += y[:, s * LANES:(s + 1) * LANES]


def _ffn(xs_tiles, items, w_in, bg, bu, w_out, bo, layer, tm):
    rows = xs_tiles.shape[0]
    n_items = items[0].shape[0]
    d, f2 = w_in.shape[2], w_in.shape[3]
    f = f2 // 2
    pw = 2 * LANES
    src = jnp.arange(pw)[:, None]
    dst = jnp.arange(pw)[None, :]
    perm = (src == jnp.where(dst < LANES, 2 * dst, 2 * (dst - LANES) + 1)).astype(BF16)
    wspec = lambda shape: pl.BlockSpec(
        shape, lambda i, tile, ex, lo, hi, fi, ne: (layer, ex[i], 0, 0))
    tspec = pl.BlockSpec((tm * ROW_TILES, LANES), lambda i, tile, ex, lo, hi, fi, ne: (tile[i], 0))
    grid_spec = pltpu.PrefetchScalarGridSpec(
        num_scalar_prefetch=6,
        grid=(n_items,),
        in_specs=[
            tspec,
            wspec((None, None, d, f2)),
            wspec((None, None, 1, f)),
            wspec((None, None, 1, f)),
            wspec((None, None, f, d)),
            wspec((None, None, 1, d)),
            pl.BlockSpec((pw, pw), lambda i, tile, ex, lo, hi, fi, ne: (0, 0)),
        ],
        out_specs=tspec,
        scratch_shapes=[
            pltpu.VMEM((d, f), BF16),
            pltpu.VMEM((d, f), BF16),
            pltpu.VMEM((f, d), BF16),
        ],
    )
    return pl.pallas_call(
        _ffn_kernel,
        grid_spec=grid_spec,
        out_shape=jax.ShapeDtypeStruct((rows, LANES), F32),
        compiler_params=_cparams(("arbitrary",)),
        name="moe_ffn",
    )(*items, xs_tiles, w_in, bg, bu, w_out, bo, perm)


def _combine_kernel(pos_ref, pos_next_ref, gates_ref, x_ref, gf_ref, pg_ref, ys_ref, o_ref,
                    buf_ref, sems):
    tm = x_ref.shape[0]
    i = pl.program_id(0)
    slot = i % 2

    def gather(p_ref, dst_slot):
        def issue(r, carry):
            for k in range(TOP_K):
                _row_copy(ys_ref, p_ref[r * TOP_K + k], buf_ref.at[dst_slot], k * tm + r,
                          sems.at[dst_slot]).start()
            return carry
        lax.fori_loop(0, tm, issue, 0)

    @pl.when(i == 0)
    def _():
        gather(pos_ref, 0)

    @pl.when(i + 1 < pl.num_programs(0))
    def _():
        gather(pos_next_ref, 1 - slot)

    for k in range(TOP_K):
        pltpu.make_async_copy(ys_ref.at[pl.ds(0, tm * ROW_TILES)],
                              buf_ref.at[slot, pl.ds(0, tm * ROW_TILES)], sems.at[slot]).wait()

    gates = gates_ref[...]
    moe = None
    for k in range(TOP_K):
        yk = jnp.concatenate(
            [buf_ref[slot, pl.ds(k * tm * ROW_TILES + s, tm, stride=ROW_TILES), :]
             for s in range(ROW_TILES)], axis=1)
        term = gates[:, k:k + 1] * yk
        moe = term if moe is None else moe + term
    o_ref[...] = x_ref[...] + gf_ref[...] * (_rms(moe) * pg_ref[...])


def _combine(ys_tiles, pos_flat, gates, x, gate_f, post_g, seq):
    t, d = x.shape
    bsz = gate_f.shape[0]
    tm = 256
    n_steps = t // tm
    return pl.pallas_call(
        _combine_kernel,
        grid=(n_steps,),
        in_specs=[
            pl.BlockSpec((tm * TOP_K,), lambda i: (i,), memory_space=pltpu.SMEM),
            pl.BlockSpec((tm * TOP_K,), lambda i: (jnp.minimum(i + 1, n_steps - 1),),
                         memory_space=pltpu.SMEM),
            pl.BlockSpec((tm, LANES), lambda i: (i, 0)),
            pl.BlockSpec((tm, d), lambda i: (i, 0)),
            pl.BlockSpec((None, 1, d), lambda i: (i * tm // seq, 0, 0)),
            pl.BlockSpec((1, d), lambda i: (0, 0)),
            pl.BlockSpec(memory_space=pl.ANY),
        ],
        out_specs=pl.BlockSpec((tm, d), lambda i: (i, 0)),
        out_shape=jax.ShapeDtypeStruct((t, d), F32),
        scratch_shapes=[
            pltpu.VMEM((2, TOP_K * tm * ROW_TILES, LANES), F32),
            pltpu.SemaphoreType.DMA((2,)),
        ],
        compiler_params=_cparams(("arbitrary",)),
        name="moe_combine",
    )(pos_flat, pos_flat, gates, x, gate_f.reshape(bsz, 1, d), post_g.reshape(1, d), ys_tiles)


def _ffn_items(counts, n_rows, tm):
    n_tiles = n_rows // tm
    n_items = n_tiles + N_EXPERTS - 1
    off = jnp.concatenate([jnp.zeros((1,), jnp.int32), jnp.cumsum(counts).astype(jnp.int32)])
    first_tile = off[:-1] // tm
    last_tile = (off[1:] - 1) // tm
    per = jnp.where(counts > 0, last_tile - first_tile + 1, 0)
    istart = jnp.concatenate([jnp.zeros((1,), jnp.int32), jnp.cumsum(per).astype(jnp.int32)])
    total = istart[-1]
    i = jnp.arange(n_items, dtype=jnp.int32)
    e = jnp.sum((istart[None, :] <= i[:, None]).astype(jnp.int32), axis=1) - 1
    e = jnp.clip(e, 0, N_EXPERTS - 1)
    valid = i < total
    e = jnp.where(valid, e, e[jnp.maximum(total - 1, 0)])
    tile = jnp.where(valid, first_tile[e] + (i - istart[e]), n_tiles - 1)
    lo = jnp.where(valid, jnp.clip(off[e] - tile * tm, 0, tm), 0)
    hi = jnp.where(valid, jnp.clip(off[e + 1] - tile * tm, 0, tm), 0)
    prev_tile = jnp.concatenate([jnp.full((1,), -1, jnp.int32), tile[:-1]])
    first = (tile != prev_tile).astype(jnp.int32)
    prev_e = jnp.concatenate([jnp.full((1,), -1, jnp.int32), e[:-1]])
    new_expert = (e != prev_e).astype(jnp.int32)
    return off, (tile.astype(jnp.int32), e.astype(jnp.int32), lo.astype(jnp.int32),
                 hi.astype(jnp.int32), first, new_expert)


def _moe_layer(x, pre_g, scale, shift, gate_f, post_g, router_w, router_b,
               w_in, b_in, w_out, b_out, layer, seq):
    t = x.shape[0]
    depth = w_in.shape[0]
    ffn_tm = 512
    h_tiles, idx, gates, rank, lpos, tcnt = _router(x, pre_g, scale, shift, router_w, router_b, seq)
    tile_cnt = tcnt.reshape(t // ROUTE_TM, SUBLANES, LANES)[:, 0, :N_EXPERTS]
    counts = jnp.sum(tile_cnt, axis=0)
    off, items = _ffn_items(counts, t * TOP_K, ffn_tm)
    pos = (off[idx[:, :TOP_K]] + rank[:, :TOP_K]).astype(jnp.int32).reshape(t * TOP_K)
    xs = _dispatch(h_tiles, lpos[:, :TOP_K].reshape(t * TOP_K), tile_cnt, off)
    bg = b_in[:, :, 0::2].reshape(depth, N_EXPERTS, 1, -1)
    bu = b_in[:, :, 1::2].reshape(depth, N_EXPERTS, 1, -1)
    ys = _ffn(xs, items, w_in, bg, bu, w_out, b_out.reshape(depth, N_EXPERTS, 1, -1), layer, ffn_tm)
    return _combine(ys, pos, gates, x, gate_f, post_g, seq)


def kernel(x, c, ada_w, ada_b, mix_pre_g, mix_post_g, ffn_pre_g, ffn_post_g, ssd_in_w, ssd_conv_w,
           ssd_conv_b, ssd_dt_bias, ssd_a_log, ssd_d, ssd_norm_g, ssd_out_w, kv_norm_g, kv_w, sb_q_w,
           sb_o_w, router_w, router_b, exp_w_in, exp_b_in, exp_w_out, exp_b_out):
    bsz, seq, d = x.shape
    t = bsz * seq
    xf = x.reshape(t, d)
    mod = _ada_mod(c, ada_w, ada_b)

    sh_m, sc_m, g_m, sh_f, sc_f, g_f = [mod[0, :, k * d:(k + 1) * d] for k in range(6)]
    in_w = ssd_in_w[0]
    w_z = in_w[:, :D_INNER].astype(BF16)
    w_xbc = in_w[:, D_INNER:D_INNER + CONV_DIM].astype(BF16)
    w_dt = jnp.pad(in_w[:, D_INNER + CONV_DIM:], ((0, 0), (0, LANES - SSD_HEADS))).astype(BF16)
    z, xbc, dt_raw = _norm_mm(xf, mix_pre_g[0], sc_m, sh_m, [w_z, w_xbc, w_dt],
                              [BF16, BF16, F32], seq, "ssd_in_proj")
    y = _ssd(z, xbc, dt_raw, ssd_conv_w[0], ssd_conv_b[0], ssd_dt_bias[0], ssd_a_log[0],
             ssd_d[0], ssd_norm_g[0], bsz, seq)
    xf = _mm_resid(y, ssd_out_w[0].astype(BF16), xf, g_m, mix_post_g[0], seq, "ssd_out_proj")
    xf = _moe_layer(xf, ffn_pre_g[0], sc_f, sh_f, g_f, ffn_post_g[0], router_w[0], router_b[0],
                    exp_w_in, exp_b_in, exp_w_out, exp_b_out, 0, seq)

    sh_m, sc_m, g_m, sh_f, sc_f, g_f = [mod[1, :, k * d:(k + 1) * d] for k in range(6)]
    kv, q = _kvq_proj(xf, kv_norm_g, mix_pre_g[1], sc_m, sh_m, kv_w.astype(BF16),
                      sb_q_w[0].astype(BF16), seq)
    att = _attention(q, kv, bsz, seq)
    xf = _mm_resid(att, sb_o_w[0].astype(BF16), xf, g_m, mix_post_g[1], seq, "attn_out_proj")
    xf = _moe_layer(xf, ffn_pre_g[1], sc_f, sh_f, g_f, ffn_post_g[1], router_w[1], router_b[1],
                    exp_w_in, exp_b_in, exp_w_out, exp_b_out, 1, seq)
    return xf.reshape(bsz, seq, d)
```

```python
import functools

import jax
import jax.numpy as jnp
from jax import lax
from jax.experimental import pallas as pl
from jax.experimental.pallas import tpu as pltpu

F32 = jnp.float32
BF16 = jnp.bfloat16

D_MODEL = 1024
D_INNER = 2048
SSD_HEADDIM = 64
SSD_HEADS = 32
SSD_GROUPS = 4
SSD_STATE = 128
SSD_CONV = 4
SSD_CHUNK = 128
CONV_DIM = D_INNER + 2 * SSD_GROUPS * SSD_STATE
SB_HEADS = 16
SB_HEADDIM = 64
N_EXPERTS = 32
TOP_K = 4
SWIGLU_LIMIT = 7.0
SWIGLU_ALPHA = 1.702
NORM_EPS = 1e-6

LANES = 128
SUBLANES = 8
ROW_TILES = D_MODEL // LANES
VMEM_LIMIT = 56 * 1024 * 1024

EXP_UNDERFLOW = -88.0

ROUTE_TM = 512


def _cparams(sem):
    return pltpu.CompilerParams(dimension_semantics=sem, vmem_limit_bytes=VMEM_LIMIT)


def _softplus(x):
    return jnp.maximum(x, 0.0) + jnp.log(1.0 + jnp.exp(-jnp.abs(x)))


def _sigmoid(x):
    return 0.5 * jnp.tanh(0.5 * x) + 0.5


def _rms(x):
    return x * lax.rsqrt(jnp.mean(x * x, axis=-1, keepdims=True) + NORM_EPS)


def _ada_kernel(c_ref, w_ref, b_ref, o_ref):
    c = c_ref[...]
    ca = (c * _sigmoid(c)).astype(BF16)
    o_ref[...] = jnp.dot(ca, w_ref[...].astype(BF16), preferred_element_type=F32) + b_ref[...]


def _ada_mod(c, ada_w, ada_b):
    depth, d, n = ada_w.shape
    bsz = c.shape[0]
    tn = 1024
    return pl.pallas_call(
        _ada_kernel,
        grid=(depth, n // tn),
        in_specs=[
            pl.BlockSpec((bsz, d), lambda l, j: (0, 0)),
            pl.BlockSpec((None, d, tn), lambda l, j: (l, 0, j)),
            pl.BlockSpec((None, 1, tn), lambda l, j: (l, 0, j)),
        ],
        out_specs=pl.BlockSpec((None, bsz, tn), lambda l, j: (l, 0, j)),
        out_shape=jax.ShapeDtypeStruct((depth, bsz, n), F32),
        compiler_params=_cparams(("parallel", "parallel")),
        name="ada_mod",
    )(c, ada_w, ada_b.reshape(depth, 1, n))


def _norm_mm_kernel(x_ref, g_ref, sc_ref, sh_ref, *refs, n_w, col_chunk):
    w_refs, o_refs = refs[:n_w], refs[n_w:]
    h = _rms(x_ref[...]) * g_ref[...]
    h = h * (1.0 + sc_ref[...]) + sh_ref[...]
    hb = h.astype(BF16)
    for w_ref, o_ref in zip(w_refs, o_refs):
        n = w_ref.shape[1]
        for c0 in range(0, n, col_chunk):
            cw = min(col_chunk, n - c0)
            o_ref[:, c0:c0 + cw] = jnp.dot(
                hb, w_ref[:, c0:c0 + cw], preferred_element_type=F32).astype(o_ref.dtype)


def _norm_mm(x, g, scale, shift, weights, out_dtypes, seq, name):
    t, d = x.shape
    bsz = scale.shape[0]
    tm = 256
    n_w = len(weights)
    in_specs = [
        pl.BlockSpec((tm, d), lambda i: (i, 0)),
        pl.BlockSpec((1, d), lambda i: (0, 0)),
        pl.BlockSpec((None, 1, d), lambda i: (i * tm // seq, 0, 0)),
        pl.BlockSpec((None, 1, d), lambda i: (i * tm // seq, 0, 0)),
    ] + [pl.BlockSpec(w.shape, lambda i: (0, 0)) for w in weights]
    out_specs = [pl.BlockSpec((tm, w.shape[1]), lambda i: (i, 0)) for w in weights]
    out_shape = [jax.ShapeDtypeStruct((t, w.shape[1]), dt) for w, dt in zip(weights, out_dtypes)]
    return pl.pallas_call(
        functools.partial(_norm_mm_kernel, n_w=n_w, col_chunk=512),
        grid=(t // tm,),
        in_specs=in_specs,
        out_specs=out_specs,
        out_shape=out_shape,
        compiler_params=_cparams(("parallel",)),
        name=name,
    )(x, g.reshape(1, d), scale.reshape(bsz, 1, d), shift.reshape(bsz, 1, d), *weights)


def _kvq_kernel(x_ref, kvg_ref, g_ref, sc_ref, sh_ref, kvw_ref, qw_ref, kv_ref, q_ref, *, col_chunk):
    r = _rms(x_ref[...])
    hk = (r * kvg_ref[...]).astype(BF16)
    hq = ((r * g_ref[...]) * (1.0 + sc_ref[...]) + sh_ref[...]).astype(BF16)
    for hb, w_ref, o_ref in ((hk, kvw_ref, kv_ref), (hq, qw_ref, q_ref)):
        for c0 in range(0, w_ref.shape[1], col_chunk):
            o_ref[:, c0:c0 + col_chunk] = jnp.dot(
                hb, w_ref[:, c0:c0 + col_chunk], preferred_element_type=F32).astype(o_ref.dtype)


def _kvq_proj(x, kv_g, g, scale, shift, kv_w, q_w, seq):
    t, d = x.shape
    bsz = scale.shape[0]
    tm = 512
    row = lambda i: (i, 0)
    const = lambda i: (0, 0)
    per_batch = lambda i: (i * tm // seq, 0, 0)
    return pl.pallas_call(
        functools.partial(_kvq_kernel, col_chunk=512),
        grid=(t // tm,),
        in_specs=[
            pl.BlockSpec((tm, d), row),
            pl.BlockSpec((1, d), const),
            pl.BlockSpec((1, d), const),
            pl.BlockSpec((None, 1, d), per_batch),
            pl.BlockSpec((None, 1, d), per_batch),
            pl.BlockSpec(kv_w.shape, const),
            pl.BlockSpec(q_w.shape, const),
        ],
        out_specs=[pl.BlockSpec((tm, kv_w.shape[1]), row), pl.BlockSpec((tm, q_w.shape[1]), row)],
        out_shape=[jax.ShapeDtypeStruct((t, kv_w.shape[1]), BF16),
                   jax.ShapeDtypeStruct((t, q_w.shape[1]), BF16)],
        compiler_params=_cparams(("parallel",)),
        name="kv_q_proj",
    )(x, kv_g.reshape(1, d), g.reshape(1, d), scale.reshape(bsz, 1, d), shift.reshape(bsz, 1, d),
      kv_w, q_w)


def _mm_resid_kernel(y_ref, w_ref, x_ref, gate_ref, pg_ref, o_ref):
    y = jnp.dot(y_ref[...], w_ref[...], preferred_element_type=F32)
    o_ref[...] = x_ref[...] + gate_ref[...] * (_rms(y) * pg_ref[...])


def _mm_resid(y, w, x, gate, post_g, seq, name):
    t, k = y.shape
    d = w.shape[1]
    bsz = gate.shape[0]
    tm = 512
    return pl.pallas_call(
        _mm_resid_kernel,
        grid=(t // tm,),
        in_specs=[
            pl.BlockSpec((tm, k), lambda i: (i, 0)),
            pl.BlockSpec((k, d), lambda i: (0, 0)),
            pl.BlockSpec((tm, d), lambda i: (i, 0)),
            pl.BlockSpec((None, 1, d), lambda i: (i * tm // seq, 0, 0)),
            pl.BlockSpec((1, d), lambda i: (0, 0)),
        ],
        out_specs=pl.BlockSpec((tm, d), lambda i: (i, 0)),
        out_shape=jax.ShapeDtypeStruct((t, d), F32),
        compiler_params=_cparams(("parallel",)),
        name=name,
    )(y, w, x, gate.reshape(bsz, 1, d), post_g.reshape(1, d))


def _ssd_kernel(z_ref, xbc_ref, dt_ref, cw_ref, cb_ref, dtb_ref, alog_ref, dsk_ref, ng_ref, e_ref,
                y_ref, ext_ref, state_ref, act_ref, yacc_ref):
    L = SSD_CHUNK
    c = pl.program_id(1)

    @pl.when(c == 0)
    def _():
        ext_ref[0:SUBLANES, :] = jnp.zeros((SUBLANES, CONV_DIM), F32)
        state_ref[...] = jnp.zeros(state_ref.shape, F32)

    ext_ref[SUBLANES:SUBLANES + L, :] = xbc_ref[...].astype(F32)
    cc = 512
    for c0 in range(0, CONV_DIM, cc):
        acc = cb_ref[:, c0:c0 + cc] + ext_ref[SUBLANES:SUBLANES + L, c0:c0 + cc] * cw_ref[3:4, c0:c0 + cc]
        for j in range(SSD_CONV - 1):
            r0 = SUBLANES - (SSD_CONV - 1) + j
            acc = acc + ext_ref[r0:r0 + L, c0:c0 + cc] * cw_ref[j:j + 1, c0:c0 + cc]
        act_ref[:, c0:c0 + cc] = acc * _sigmoid(acc)
    ext_ref[0:SUBLANES, :] = ext_ref[L:L + SUBLANES, :]

    row = lax.broadcasted_iota(jnp.int32, (L, L), 0)
    col = lax.broadcasted_iota(jnp.int32, (L, L), 1)
    causal = col <= row
    lane_lo = col < SSD_HEADDIM

    dt = _softplus(dt_ref[...] + dtb_ref[...])
    da = dt * (-jnp.exp(alog_ref[...]))
    ltri = jnp.where(causal, 1.0, 0.0).astype(BF16)
    da_hi = da.astype(BF16)
    da_lo = (da - da_hi.astype(F32)).astype(BF16)
    a_cum = (jnp.dot(ltri, da_hi, preferred_element_type=F32)
             + jnp.dot(ltri, da_lo, preferred_element_type=F32))
    a_cum_t = a_cum.T
    exp_a = jnp.exp(a_cum)
    decay = jnp.exp(a_cum[L - 1:L, :] - a_cum)

    e = e_ref[...]
    dt_x = jnp.dot(dt.astype(BF16), e, preferred_element_type=F32)
    exp_a_x = jnp.dot(exp_a.astype(BF16), e, preferred_element_type=F32)
    decay_x = jnp.dot(decay.astype(BF16), e, preferred_element_type=F32)

    gn = SSD_GROUPS * SSD_STATE
    pairs_per_group = SSD_HEADS // SSD_GROUPS // 2
    for g in range(SSD_GROUPS):
        b_g = act_ref[:, D_INNER + g * SSD_STATE:D_INNER + (g + 1) * SSD_STATE]
        c_g = act_ref[:, D_INNER + gn + g * SSD_STATE:D_INNER + gn + (g + 1) * SSD_STATE].astype(BF16)
        b_gt = b_g.T.astype(BF16)
        cb = jnp.dot(c_g, b_gt, preferred_element_type=F32)
        for pp in range(pairs_per_group):
            p = g * pairs_per_group + pp
            sl = slice(p * LANES, (p + 1) * LANES)
            xs_p = act_ref[:, sl]
            xdt = xs_p * dt_x[:, sl]
            xdt_b = xdt.astype(BF16)
            w_b = (xdt * decay_x[:, sl]).astype(BF16)
            yd = []
            for i in range(2):
                h = 2 * p + i
                seg = a_cum[:, h:h + 1] - a_cum_t[h:h + 1, :]
                lm = jnp.where(causal, jnp.exp(jnp.minimum(seg, 0.0)), 0.0)
                m = (cb * lm).astype(BF16)
                yd.append(jnp.dot(m, xdt_b, preferred_element_type=F32))
            y_diag = jnp.where(lane_lo, yd[0], yd[1])
            prev_t = state_ref[p]
            y_off = jnp.dot(c_g, prev_t.astype(BF16), preferred_element_type=F32) * exp_a_x[:, sl]
            s_t = jnp.dot(b_gt, w_b, preferred_element_type=F32)
            state_ref[p] = prev_t * exp_a_x[L - 1:L, sl] + s_t
            yacc_ref[:, sl] = y_diag + y_off + xs_p * dsk_ref[:, sl]

    gw = D_INNER // SSD_GROUPS
    for g in range(SSD_GROUPS):
        sl = slice(g * gw, (g + 1) * gw)
        zf = z_ref[:, sl].astype(F32)
        gated = yacc_ref[:, sl] * (zf * _sigmoid(zf))
        y_ref[:, sl] = (_rms(gated) * ng_ref[:, sl]).astype(y_ref.dtype)


def _ssd(z, xbc, dt_raw, conv_w, conv_b, dt_bias, a_log, d_skip, norm_g, bsz, seq):
    t = z.shape[0]
    nc = seq // SSD_CHUNK
    L = SSD_CHUNK
    pad = LANES - SSD_HEADS
    dtb = jnp.pad(dt_bias, (0, pad)).reshape(1, LANES)
    alog = jnp.pad(a_log, (0, pad)).reshape(1, LANES)
    dsk = jnp.repeat(d_skip, SSD_HEADDIM).reshape(1, D_INNER)
    expand = (jnp.arange(LANES)[:, None] == (jnp.arange(D_INNER)[None, :] // SSD_HEADDIM)).astype(BF16)
    full = lambda shape: pl.BlockSpec(shape, lambda b, c: (0, 0))
    return pl.pallas_call(
        _ssd_kernel,
        grid=(bsz, nc),
        in_specs=[
            pl.BlockSpec((L, D_INNER), lambda b, c: (b * nc + c, 0)),
            pl.BlockSpec((L, CONV_DIM), lambda b, c: (b * nc + c, 0)),
            pl.BlockSpec((L, LANES), lambda b, c: (b * nc + c, 0)),
            full((SSD_CONV, CONV_DIM)),
            full((1, CONV_DIM)),
            full((1, LANES)),
            full((1, LANES)),
            full((1, D_INNER)),
            full((1, D_INNER)),
            full((LANES, D_INNER)),
        ],
        out_specs=pl.BlockSpec((L, D_INNER), lambda b, c: (b * nc + c, 0)),
        out_shape=jax.ShapeDtypeStruct((t, D_INNER), BF16),
        scratch_shapes=[
            pltpu.VMEM((L + 2 * SUBLANES, CONV_DIM), F32),
            pltpu.VMEM((SSD_HEADS // 2, SSD_STATE, LANES), F32),
            pltpu.VMEM((L, CONV_DIM), F32),
            pltpu.VMEM((L, D_INNER), F32),
        ],
        compiler_params=_cparams(("arbitrary", "arbitrary")),
        name="ssd_scan",
    )(z, xbc, dt_raw, conv_w, conv_b.reshape(1, CONV_DIM), dtb, alog, dsk,
      norm_g.reshape(1, D_INNER), expand)


def _attn_kernel(q_ref, k_ref, v_ref, o_ref, carry_ref, acc_ref):
    bq = q_ref.shape[0]
    qi = pl.program_id(2)
    scale = SB_HEADDIM ** -0.5
    row = lax.broadcasted_iota(jnp.int32, (bq, bq), 0)
    col = lax.broadcasted_iota(jnp.int32, (bq, bq), 1)
    strict = col < row
    tail_sum = jnp.concatenate(
        [jnp.where(row > col, -1.0, 0.0), jnp.full((bq, bq), -1.0, F32)], axis=1).astype(BF16)
    lane_lo = lax.broadcasted_iota(jnp.int32, (bq, LANES), 1) < SB_HEADDIM
    n_pairs = q_ref.shape[1] // LANES
    heads = [(p, i) for p in range(n_pairs) for i in range(2)]

    carry_ref[...] = jnp.zeros(carry_ref.shape, F32)
    acc_ref[...] = jnp.zeros(acc_ref.shape, F32)

    def block(kb, diagonal):
        k0 = pl.multiple_of(kb * bq, bq)
        sls = [slice(p * LANES, (p + 1) * LANES) for p, _ in heads]
        zs = []
        for (p, i), sl in zip(heads, sls):
            q = q_ref[:, sl] * scale
            qm = jnp.where(lane_lo if i == 0 else jnp.logical_not(lane_lo), q, jnp.zeros_like(q))
            zs.append(lax.dot_general(qm, k_ref[pl.ds(k0, bq), sl], (((1,), (1,)), ((), ())),
                                      preferred_element_type=F32))
        sps = [_softplus(z) for z in zs]
        l1ms = [(jnp.where(strict, sp, 0.0) if diagonal else sp).astype(BF16) for sp in sps]
        tss = [jnp.dot(l1m, tail_sum, preferred_element_type=F32) for l1m in l1ms]
        crs = [carry_ref[n] for n in range(len(heads))]
        a_s = []
        for z, sp, ts, cr in zip(zs, sps, tss, crs):
            a = jnp.exp((z - sp) + ts[:, :bq] + cr)
            if diagonal:
                a = jnp.where(strict, a, 0.0)
            a_s.append(a.astype(BF16))
        pvs = [jnp.dot(a, v_ref[pl.ds(k0, bq), sl], preferred_element_type=F32)
               for a, sl in zip(a_s, sls)]
        cmax = None
        for n in range(len(heads)):
            acc_ref[n] += pvs[n]
            cnew = crs[n] + tss[n][:, bq:]
            carry_ref[n] = cnew
            cmax = cnew if cmax is None else jnp.maximum(cmax, cnew)
        return (jnp.max(cmax) > EXP_UNDERFLOW).astype(jnp.int32)

    go0 = block(qi, True)

    def cond(st):
        kb, go = st
        return jnp.logical_and(kb >= 0, go > 0)

    def body(st):
        kb, _ = st
        return kb - 1, block(kb, False)

    lax.while_loop(cond, body, (qi - 1, go0))
    for p in range(n_pairs):
        o_ref[:, p * LANES:(p + 1) * LANES] = jnp.where(
            lane_lo, acc_ref[2 * p], acc_ref[2 * p + 1]).astype(o_ref.dtype)


def _attention(q, kv, bsz, seq):
    t = q.shape[0]
    bq = 128
    nq = seq // bq
    pairs_per_step = 8
    width = pairs_per_step * LANES
    n_groups = D_MODEL // width
    return pl.pallas_call(
        _attn_kernel,
        grid=(bsz, n_groups, nq),
        in_specs=[
            pl.BlockSpec((bq, width), lambda b, p, i: (b * nq + i, p)),
            pl.BlockSpec((seq, width), lambda b, p, i: (b, p)),
            pl.BlockSpec((seq, width), lambda b, p, i: (b, n_groups + p)),
        ],
        out_specs=pl.BlockSpec((bq, width), lambda b, p, i: (b * nq + i, p)),
        out_shape=jax.ShapeDtypeStruct((t, D_MODEL), BF16),
        scratch_shapes=[
            pltpu.VMEM((2 * pairs_per_step, bq, LANES), F32),
            pltpu.VMEM((2 * pairs_per_step, bq, LANES), F32),
        ],
        compiler_params=_cparams(("parallel", "parallel", "arbitrary")),
        name="sb_attention",
    )(q, kv, kv)


def _router_kernel(x_ref, g_ref, sc_ref, sh_ref, rw_ref, rb_ref,
                   h_ref, gate_ref, lpos_ref, tcnt_ref):
    tm = x_ref.shape[0]
    h = _rms(x_ref[...]) * g_ref[...]
    h = h * (1.0 + sc_ref[...]) + sh_ref[...]
    for s in range(ROW_TILES):
        h_ref[pl.ds(s, tm, stride=ROW_TILES), :] = h[:, s * LANES:(s + 1) * LANES]

    w = rw_ref[...]
    h_hi = h.astype(BF16)
    h_lo = (h - h_hi.astype(F32)).astype(BF16)
    w_hi = w.astype(BF16)
    w_lo = (w - w_hi.astype(F32)).astype(BF16)
    logits = (jnp.dot(h_hi, w_hi, preferred_element_type=F32)
              + jnp.dot(h_hi, w_lo, preferred_element_type=F32)
              + jnp.dot(h_lo, w_hi, preferred_element_type=F32)) + rb_ref[...]

    lane = lax.broadcasted_iota(jnp.int32, (tm, LANES), 1).astype(F32)
    work = logits
    vals, idxs = [], []
    chosen = jnp.zeros((tm, LANES), F32)
    for _ in range(TOP_K):
        m = jnp.max(work, axis=1, keepdims=True)
        am = jnp.min(jnp.where(work == m, lane, float(LANES)), axis=1, keepdims=True)
        hit = lane == am
        vals.append(m)
        idxs.append(am)
        chosen = jnp.where(hit, 1.0, chosen)
        work = jnp.where(hit, -jnp.inf, work)
    es = [jnp.exp(v - vals[0]) for v in vals]
    denom = es[0] + es[1] + es[2] + es[3]

    r = lax.broadcasted_iota(jnp.int32, (tm, tm), 0)
    cidx = lax.broadcasted_iota(jnp.int32, (tm, tm), 1)
    lstrict = jnp.where(cidx < r, 1.0, 0.0).astype(BF16)
    before_tile = jnp.dot(lstrict, chosen.astype(BF16), preferred_element_type=F32)

    tile_cnt = jnp.sum(chosen, axis=0, keepdims=True)
    cnt_hi = jnp.floor(tile_cnt * (1.0 / 16.0))
    cnt_lo = tile_cnt - 16.0 * cnt_hi
    er = lax.broadcasted_iota(jnp.int32, (LANES, LANES), 0)
    ec = lax.broadcasted_iota(jnp.int32, (LANES, LANES), 1)
    below = jnp.where(er < ec, 1.0, 0.0).astype(BF16)
    bc = lambda v: jnp.broadcast_to(v, (SUBLANES, LANES)).astype(BF16)
    seg_start = (16.0 * jnp.dot(bc(cnt_hi), below, preferred_element_type=F32)
                 + jnp.dot(bc(cnt_lo), below, preferred_element_type=F32))[0:1, :]
    local = before_tile + seg_start

    gate_out = jnp.zeros((tm, LANES), F32)
    lpos_out = jnp.zeros((tm, LANES), jnp.int32)
    for k in range(TOP_K):
        lp = jnp.sum(jnp.where(lane == idxs[k], local, 0.0), axis=1, keepdims=True)
        gate_out = jnp.where(lane == k, es[k] / denom, gate_out)
        lpos_out = jnp.where(lane == k, lp.astype(jnp.int32), lpos_out)
    gate_ref[...] = gate_out
    lpos_ref[...] = lpos_out
    tcnt_ref[...] = jnp.broadcast_to(tile_cnt, tcnt_ref.shape).astype(jnp.int32)


def _router(x, g, scale, shift, router_w, router_b, seq):
    t, d = x.shape
    bsz = scale.shape[0]
    tm = ROUTE_TM
    pad = LANES - N_EXPERTS
    rw = jnp.pad(router_w, ((0, 0), (0, pad)))
    rb = jnp.pad(router_b, (0, pad), constant_values=-1e30).reshape(1, LANES)
    return pl.pallas_call(
        _router_kernel,
        grid=(t // tm,),
        in_specs=[
            pl.BlockSpec((tm, d), lambda i: (i, 0)),
            pl.BlockSpec((1, d), lambda i: (0, 0)),
            pl.BlockSpec((None, 1, d), lambda i: (i * tm // seq, 0, 0)),
            pl.BlockSpec((None, 1, d), lambda i: (i * tm // seq, 0, 0)),
            pl.BlockSpec((d, LANES), lambda i: (0, 0)),
            pl.BlockSpec((1, LANES), lambda i: (0, 0)),
        ],
        out_specs=[
            pl.BlockSpec((tm * ROW_TILES, LANES), lambda i: (i, 0)),
            pl.BlockSpec((tm, LANES), lambda i: (i, 0)),
            pl.BlockSpec((tm, LANES), lambda i: (i, 0)),
            pl.BlockSpec((SUBLANES, LANES), lambda i: (i, 0)),
        ],
        out_shape=[
            jax.ShapeDtypeStruct((t * ROW_TILES, LANES), F32),
            jax.ShapeDtypeStruct((t, LANES), F32),
            jax.ShapeDtypeStruct((t, LANES), jnp.int32),
            jax.ShapeDtypeStruct((t // tm * SUBLANES, LANES), jnp.int32),
        ],
        compiler_params=_cparams(("parallel",)),
        name="moe_router",
    )(x, g.reshape(1, d), scale.reshape(bsz, 1, d), shift.reshape(bsz, 1, d), rw, rb)


def _segment_copies(tile, cnt_ref, start_ref, base_ref, make_copy):
    top_bit = ROUTE_TM.bit_length() - 1

    def segment(e, carry):
        n = cnt_ref[tile * N_EXPERTS + e]
        local = start_ref[tile * N_EXPERTS + e]
        glob = base_ref[tile * N_EXPERTS + e]
        for b in range(top_bit, -1, -1):
            done = (n >> (b + 1)) << (b + 1)

            @pl.when(((n >> b) & 1) == 1)
            def _():
                make_copy(pl.multiple_of((local + done) * ROW_TILES, ROW_TILES),
                          pl.multiple_of((glob + done) * ROW_TILES, ROW_TILES),
                          (1 << b) * ROW_TILES).start()
        return carry

    lax.fori_loop(0, N_EXPERTS, segment, 0)


def _segment_tables(tile_cnt, off):
    n_tiles = tile_cnt.shape[0]
    zeros_col = jnp.zeros((n_tiles, 1), jnp.int32)
    seg_start = jnp.concatenate([zeros_col, jnp.cumsum(tile_cnt, axis=1)[:, :-1]], axis=1)
    before = jnp.concatenate(
        [jnp.zeros((1, N_EXPERTS), jnp.int32), jnp.cumsum(tile_cnt, axis=0)[:-1]], axis=0)
    base = off[None, :N_EXPERTS] + before
    flat = lambda a: a.astype(jnp.int32).reshape(n_tiles * N_EXPERTS)
    return flat(tile_cnt), flat(seg_start), flat(base)


def _dispatch_kernel(cnt_ref, start_ref, base_ref, lpos_ref, h_ref, xs_ref, loc_ref, sems):
    tm = h_ref.shape[0] // ROW_TILES
    i = pl.program_id(0)
    slot = i % 2
    slot_rows = tm * TOP_K * ROW_TILES
    unroll = 4

    def wait_slot(s):
        pltpu.make_async_copy(loc_ref.at[s], xs_ref.at[pl.ds(0, slot_rows)], sems.at[s]).wait()

    def place(j, carry):
        for u in range(unroll):
            t = j * unroll + u
            v = h_ref[pl.ds(pl.multiple_of(t * ROW_TILES, ROW_TILES), ROW_TILES), :]
            for k in range(TOP_K):
                p = lpos_ref[t * TOP_K + k]
                loc_ref[slot, pl.ds(pl.multiple_of(p * ROW_TILES, ROW_TILES), ROW_TILES), :] = v
        return carry

    lax.fori_loop(0, tm // unroll, place, 0)

    _segment_copies(i, cnt_ref, start_ref, base_ref,
                    lambda loc_row, glob_row, size: pltpu.make_async_copy(
                        loc_ref.at[slot, pl.ds(loc_row, size)], xs_ref.at[pl.ds(glob_row, size)],
                        sems.at[slot]))

    @pl.when(i > 0)
    def _():
        wait_slot(1 - slot)

    @pl.when(i == pl.num_programs(0) - 1)
    def _():
        wait_slot(slot)


def _dispatch(h_tiles, lpos_flat, tables):
    rows = h_tiles.shape[0]
    t = rows // ROW_TILES
    tm = ROUTE_TM
    n_tiles = t // tm
    grid_spec = pltpu.PrefetchScalarGridSpec(
        num_scalar_prefetch=3,
        grid=(n_tiles,),
        in_specs=[
            pl.BlockSpec((tm * TOP_K,), lambda i, c, s, b: (i,), memory_space=pltpu.SMEM),
            pl.BlockSpec((tm * ROW_TILES, LANES), lambda i, c, s, b: (i, 0)),
        ],
        out_specs=pl.BlockSpec(memory_space=pl.ANY),
        scratch_shapes=[
            pltpu.VMEM((2, tm * TOP_K * ROW_TILES, LANES), F32),
            pltpu.SemaphoreType.DMA((2,)),
        ],
    )
    return pl.pallas_call(
        _dispatch_kernel,
        grid_spec=grid_spec,
        out_shape=jax.ShapeDtypeStruct((rows * TOP_K, LANES), F32),
        compiler_params=_cparams(("arbitrary",)),
        name="moe_dispatch",
    )(*tables, lpos_flat, h_tiles)


def _ffn_kernel(tile_ref, exp_ref, lo_ref, hi_ref, first_ref, newexp_ref,
                xs_ref, win_ref, bg_ref, bu_ref, wout_ref, bo_ref, perm_ref, ys_ref,
                wg_ref, wu_ref, wo_ref):
    del tile_ref, exp_ref
    i = pl.program_id(0)
    tm = xs_ref.shape[0] // ROW_TILES
    lo, hi = lo_ref[i], hi_ref[i]

    @pl.when(newexp_ref[i] == 1)
    def _():
        pw = 2 * LANES
        for cb in range(win_ref.shape[1] // pw):
            blk = win_ref[:, cb * pw:(cb + 1) * pw].astype(BF16)
            res = jnp.dot(blk, perm_ref[...], preferred_element_type=F32).astype(BF16)
            wg_ref[:, cb * LANES:(cb + 1) * LANES] = res[:, :LANES]
            wu_ref[:, cb * LANES:(cb + 1) * LANES] = res[:, LANES:]
        wo_ref[...] = wout_ref[...].astype(BF16)

    def compute():
        x = jnp.concatenate(
            [xs_ref[pl.ds(s, tm, stride=ROW_TILES), :] for s in range(ROW_TILES)], axis=1).astype(BF16)
        gate = jnp.dot(x, wg_ref[...], preferred_element_type=F32) + bg_ref[...]
        up = jnp.dot(x, wu_ref[...], preferred_element_type=F32) + bu_ref[...]
        gate = jnp.minimum(gate, SWIGLU_LIMIT)
        up = jnp.clip(up, -SWIGLU_LIMIT, SWIGLU_LIMIT)
        act = (up + 1.0) * gate * _sigmoid(SWIGLU_ALPHA * gate)
        y = jnp.dot(act.astype(BF16), wo_ref[...], preferred_element_type=F32) + bo_ref[...]
        r = lax.broadcasted_iota(jnp.int32, (tm, 1), 0)
        return jnp.where(jnp.logical_and(r >= lo, r < hi), y, 0.0)

    @pl.when(first_ref[i] == 1)
    def _():
        y = compute()
        for s in range(ROW_TILES):
            ys_ref[pl.ds(s, tm, stride=ROW_TILES), :] = y[:, s * LANES:(s + 1) * LANES]

    @pl.when(jnp.logical_and(first_ref[i] == 0, hi > lo))
    def _():
        y = compute()
        for s in range(ROW_TILES):
            ys_ref[pl.ds(s, tm, stride=ROW_TILES), :] += y[:, s * LANES:(s + 1) * LANES]


def _ffn(xs_tiles, items, w_in, bg, bu, w_out, bo, layer, tm):
    rows = xs_tiles.shape[0]
    n_items = items[0].shape[0]
    d, f2 = w_in.shape[2], w_in.shape[3]
    f = f2 // 2
    pw = 2 * LANES
    src = jnp.arange(pw)[:, None]
    dst = jnp.arange(pw)[None, :]
    perm = (src == jnp.where(dst < LANES, 2 * dst, 2 * (dst - LANES) + 1)).astype(BF16)
    wspec = lambda shape: pl.BlockSpec(
        shape, lambda i, tile, ex, lo, hi, fi, ne: (layer, ex[i], 0, 0))
    tspec = pl.BlockSpec((tm * ROW_TILES, LANES), lambda i, tile, ex, lo, hi, fi, ne: (tile[i], 0))
    grid_spec = pltpu.PrefetchScalarGridSpec(
        num_scalar_prefetch=6,
        grid=(n_items,),
        in_specs=[
            tspec,
            wspec((None, None, d, f2)),
            wspec((None, None, 1, f)),
            wspec((None, None, 1, f)),
            wspec((None, None, f, d)),
            wspec((None, None, 1, d)),
            pl.BlockSpec((pw, pw), lambda i, tile, ex, lo, hi, fi, ne: (0, 0)),
        ],
        out_specs=tspec,
        scratch_shapes=[
            pltpu.VMEM((d, f), BF16),
            pltpu.VMEM((d, f), BF16),
            pltpu.VMEM((f, d), BF16),
        ],
    )
    return pl.pallas_call(
        _ffn_kernel,
        grid_spec=grid_spec,
        out_shape=jax.ShapeDtypeStruct((rows, LANES), F32),
        compiler_params=_cparams(("arbitrary",)),
        name="moe_ffn",
    )(*items, xs_tiles, w_in, bg, bu, w_out, bo, perm)


def _combine_kernel(cnt_ref, start_ref, base_ref, lpos_ref, gates_ref, x_ref, gf_ref, pg_ref,
                    ys_ref, o_ref, loc_ref, moe_ref, sems):
    tm = x_ref.shape[0]
    i = pl.program_id(0)
    slot = i % 2
    slot_rows = tm * TOP_K * ROW_TILES
    unroll = 4

    def fetch(tile, s):
        _segment_copies(tile, cnt_ref, start_ref, base_ref,
                        lambda loc_row, glob_row, size: pltpu.make_async_copy(
                            ys_ref.at[pl.ds(glob_row, size)], loc_ref.at[s, pl.ds(loc_row, size)],
                            sems.at[s]))

    @pl.when(i == 0)
    def _():
        fetch(i, 0)

    @pl.when(i + 1 < pl.num_programs(0))
    def _():
        fetch(i + 1, 1 - slot)

    pltpu.make_async_copy(ys_ref.at[pl.ds(0, slot_rows)], loc_ref.at[slot], sems.at[slot]).wait()

    def mix(j, carry):
        for u in range(unroll):
            t = j * unroll + u
            acc = None
            for k in range(TOP_K):
                p = lpos_ref[t * TOP_K + k]
                term = gates_ref[t * TOP_K + k] * loc_ref[
                    slot, pl.ds(pl.multiple_of(p * ROW_TILES, ROW_TILES), ROW_TILES), :]
                acc = term if acc is None else acc + term
            moe_ref[pl.ds(pl.multiple_of(t * ROW_TILES, ROW_TILES), ROW_TILES), :] = acc
        return carry

    lax.fori_loop(0, tm // unroll, mix, 0)

    moe = jnp.concatenate(
        [moe_ref[pl.ds(s, tm, stride=ROW_TILES), :] for s in range(ROW_TILES)], axis=1)
    o_ref[...] = x_ref[...] + gf_ref[...] * (_rms(moe) * pg_ref[...])


def _combine(ys_tiles, lpos_flat, gates_flat, tables, x, gate_f, post_g, seq):
    t, d = x.shape
    bsz = gate_f.shape[0]
    tm = ROUTE_TM
    grid_spec = pltpu.PrefetchScalarGridSpec(
        num_scalar_prefetch=3,
        grid=(t // tm,),
        in_specs=[
            pl.BlockSpec((tm * TOP_K,), lambda i, c, s, b: (i,), memory_space=pltpu.SMEM),
            pl.BlockSpec((tm * TOP_K,), lambda i, c, s, b: (i,), memory_space=pltpu.SMEM),
            pl.BlockSpec((tm, d), lambda i, c, s, b: (i, 0)),
            pl.BlockSpec((None, 1, d), lambda i, c, s, b: (i * tm // seq, 0, 0)),
            pl.BlockSpec((1, d), lambda i, c, s, b: (0, 0)),
            pl.BlockSpec(memory_space=pl.ANY),
        ],
        out_specs=pl.BlockSpec((tm, d), lambda i, c, s, b: (i, 0)),
        scratch_shapes=[
            pltpu.VMEM((2, tm * TOP_K * ROW_TILES, LANES), F32),
            pltpu.VMEM((tm * ROW_TILES, LANES), F32),
            pltpu.SemaphoreType.DMA((2,)),
        ],
    )
    return pl.pallas_call(
        _combine_kernel,
        grid_spec=grid_spec,
        out_shape=jax.ShapeDtypeStruct((t, d), F32),
        compiler_params=_cparams(("arbitrary",)),
        name="moe_combine",
    )(*tables, lpos_flat, gates_flat, x, gate_f.reshape(bsz, 1, d), post_g.reshape(1, d), ys_tiles)


def _ffn_items(counts, n_rows, tm):
    n_tiles = n_rows // tm
    n_items = n_tiles + N_EXPERTS - 1
    off = jnp.concatenate([jnp.zeros((1,), jnp.int32), jnp.cumsum(counts).astype(jnp.int32)])
    first_tile = off[:-1] // tm
    last_tile = (off[1:] - 1) // tm
    per = jnp.where(counts > 0, last_tile - first_tile + 1, 0)
    istart = jnp.concatenate([jnp.zeros((1,), jnp.int32), jnp.cumsum(per).astype(jnp.int32)])
    total = istart[-1]
    i = jnp.arange(n_items, dtype=jnp.int32)
    e = jnp.sum((istart[None, :] <= i[:, None]).astype(jnp.int32), axis=1) - 1
    e = jnp.clip(e, 0, N_EXPERTS - 1)
    valid = i < total
    e = jnp.where(valid, e, e[jnp.maximum(total - 1, 0)])
    tile = jnp.where(valid, first_tile[e] + (i - istart[e]), n_tiles - 1)
    lo = jnp.where(valid, jnp.clip(off[e] - tile * tm, 0, tm), 0)
    hi = jnp.where(valid, jnp.clip(off[e + 1] - tile * tm, 0, tm), 0)
    prev_tile = jnp.concatenate([jnp.full((1,), -1, jnp.int32), tile[:-1]])
    first = (tile != prev_tile).astype(jnp.int32)
    prev_e = jnp.concatenate([jnp.full((1,), -1, jnp.int32), e[:-1]])
    new_expert = (e != prev_e).astype(jnp.int32)
    return off, (tile.astype(jnp.int32), e.astype(jnp.int32), lo.astype(jnp.int32),
                 hi.astype(jnp.int32), first, new_expert)


def _moe_layer(x, pre_g, scale, shift, gate_f, post_g, router_w, router_b,
               w_in, b_in, w_out, b_out, layer, seq):
    t = x.shape[0]
    depth = w_in.shape[0]
    ffn_tm = 512
    h_tiles, gates, lpos, tcnt = _router(x, pre_g, scale, shift, router_w, router_b, seq)
    tile_cnt = tcnt.reshape(t // ROUTE_TM, SUBLANES, LANES)[:, 0, :N_EXPERTS]
    counts = jnp.sum(tile_cnt, axis=0)
    off, items = _ffn_items(counts, t * TOP_K, ffn_tm)
    tables = _segment_tables(tile_cnt, off)
    lpos_flat = lpos[:, :TOP_K].reshape(t * TOP_K)
    xs = _dispatch(h_tiles, lpos_flat, tables)
    bg = b_in[:, :, 0::2].reshape(depth, N_EXPERTS, 1, -1)
    bu = b_in[:, :, 1::2].reshape(depth, N_EXPERTS, 1, -1)
    ys = _ffn(xs, items, w_in, bg, bu, w_out, b_out.reshape(depth, N_EXPERTS, 1, -1), layer, ffn_tm)
    return _combine(ys, lpos_flat, gates[:, :TOP_K].reshape(t * TOP_K), tables, x, gate_f, post_g, seq)


def kernel(x, c, ada_w, ada_b, mix_pre_g, mix_post_g, ffn_pre_g, ffn_post_g, ssd_in_w, ssd_conv_w,
           ssd_conv_b, ssd_dt_bias, ssd_a_log, ssd_d, ssd_norm_g, ssd_out_w, kv_norm_g, kv_w, sb_q_w,
           sb_o_w, router_w, router_b, exp_w_in, exp_b_in, exp_w_out, exp_b_out):
    bsz, seq, d = x.shape
    t = bsz * seq
    xf = x.reshape(t, d)
    mod = _ada_mod(c, ada_w, ada_b)

    sh_m, sc_m, g_m, sh_f, sc_f, g_f = [mod[0, :, k * d:(k + 1) * d] for k in range(6)]
    in_w = ssd_in_w[0]
    w_z = in_w[:, :D_INNER].astype(BF16)
    w_xbc = in_w[:, D_INNER:D_INNER + CONV_DIM].astype(BF16)
    w_dt = jnp.pad(in_w[:, D_INNER + CONV_DIM:], ((0, 0), (0, LANES - SSD_HEADS))).astype(BF16)
    z, xbc, dt_raw = _norm_mm(xf, mix_pre_g[0], sc_m, sh_m, [w_z, w_xbc, w_dt],
                              [BF16, BF16, F32], seq, "ssd_in_proj")
    y = _ssd(z, xbc, dt_raw, ssd_conv_w[0], ssd_conv_b[0], ssd_dt_bias[0], ssd_a_log[0],
             ssd_d[0], ssd_norm_g[0], bsz, seq)
    xf = _mm_resid(y, ssd_out_w[0].astype(BF16), xf, g_m, mix_post_g[0], seq, "ssd_out_proj")
    xf = _moe_layer(xf, ffn_pre_g[0], sc_f, sh_f, g_f, ffn_post_g[0], router_w[0], router_b[0],
                    exp_w_in, exp_b_in, exp_w_out, exp_b_out, 0, seq)

    sh_m, sc_m, g_m, sh_f, sc_f, g_f = [mod[1, :, k * d:(k + 1) * d] for k in range(6)]
    kv, q = _kvq_proj(xf, kv_norm_g, mix_pre_g[1], sc_m, sh_m, kv_w.astype(BF16),
                      sb_q_w[0].astype(BF16), seq)
    att = _attention(q, kv, bsz, seq)
    xf = _mm_resid(att, sb_o_w[0].astype(BF16), xf, g_m, mix_post_g[1], seq, "attn_out_proj")
    xf = _moe_layer(xf, ffn_pre_g[1], sc_f, sh_f, g_f, ffn_post_g[1], router_w[1], router_b[1],
                    exp_w_in, exp_b_in, exp_w_out, exp_b_out, 1, seq)
    return xf.reshape(bsz, seq, d)
```

```python
import functools

import jax
import jax.numpy as jnp
from jax import lax
from jax.experimental import pallas as pl
from jax.experimental.pallas import tpu as pltpu

F32 = jnp.float32
BF16 = jnp.bfloat16

D_MODEL = 1024
D_INNER = 2048
SSD_HEADDIM = 64
SSD_HEADS = 32
SSD_GROUPS = 4
SSD_STATE = 128
SSD_CONV = 4
SSD_CHUNK = 128
CONV_DIM = D_INNER + 2 * SSD_GROUPS * SSD_STATE
SB_HEADS = 16
SB_HEADDIM = 64
N_EXPERTS = 32
TOP_K = 4
SWIGLU_LIMIT = 7.0
SWIGLU_ALPHA = 1.702
NORM_EPS = 1e-6

LANES = 128
SUBLANES = 8
ROW_TILES = D_MODEL // LANES
VMEM_LIMIT = 56 * 1024 * 1024

EXP_UNDERFLOW = -88.0

ROUTE_TM = 512


def _cparams(sem):
    return pltpu.CompilerParams(dimension_semantics=sem, vmem_limit_bytes=VMEM_LIMIT)


def _softplus(x):
    return jnp.maximum(x, 0.0) + jnp.log(1.0 + jnp.exp(-jnp.abs(x)))


def _sigmoid(x):
    return 0.5 * jnp.tanh(0.5 * x) + 0.5


def _rms(x):
    return x * lax.rsqrt(jnp.mean(x * x, axis=-1, keepdims=True) + NORM_EPS)


def _ada_kernel(c_ref, w_ref, b_ref, o_ref):
    c = c_ref[...]
    ca = (c * _sigmoid(c)).astype(BF16)
    o_ref[...] = jnp.dot(ca, w_ref[...].astype(BF16), preferred_element_type=F32) + b_ref[...]


def _ada_mod(c, ada_w, ada_b):
    depth, d, n = ada_w.shape
    bsz = c.shape[0]
    tn = 1024
    return pl.pallas_call(
        _ada_kernel,
        grid=(depth, n // tn),
        in_specs=[
            pl.BlockSpec((bsz, d), lambda l, j: (0, 0)),
            pl.BlockSpec((None, d, tn), lambda l, j: (l, 0, j)),
            pl.BlockSpec((None, 1, tn), lambda l, j: (l, 0, j)),
        ],
        out_specs=pl.BlockSpec((None, bsz, tn), lambda l, j: (l, 0, j)),
        out_shape=jax.ShapeDtypeStruct((depth, bsz, n), F32),
        compiler_params=_cparams(("parallel", "parallel")),
        name="ada_mod",
    )(c, ada_w, ada_b.reshape(depth, 1, n))


def _norm_mm_kernel(x_ref, g_ref, sc_ref, sh_ref, *refs, n_w, col_chunk):
    w_refs, o_refs = refs[:n_w], refs[n_w:]
    h = _rms(x_ref[...]) * g_ref[...]
    h = h * (1.0 + sc_ref[...]) + sh_ref[...]
    hb = h.astype(BF16)
    for w_ref, o_ref in zip(w_refs, o_refs):
        n = w_ref.shape[1]
        for c0 in range(0, n, col_chunk):
            cw = min(col_chunk, n - c0)
            o_ref[:, c0:c0 + cw] = jnp.dot(
                hb, w_ref[:, c0:c0 + cw], preferred_element_type=F32).astype(o_ref.dtype)


def _norm_mm(x, g, scale, shift, weights, out_dtypes, seq, name):
    t, d = x.shape
    bsz = scale.shape[0]
    tm = 256
    n_w = len(weights)
    in_specs = [
        pl.BlockSpec((tm, d), lambda i: (i, 0)),
        pl.BlockSpec((1, d), lambda i: (0, 0)),
        pl.BlockSpec((None, 1, d), lambda i: (i * tm // seq, 0, 0)),
        pl.BlockSpec((None, 1, d), lambda i: (i * tm // seq, 0, 0)),
    ] + [pl.BlockSpec(w.shape, lambda i: (0, 0)) for w in weights]
    out_specs = [pl.BlockSpec((tm, w.shape[1]), lambda i: (i, 0)) for w in weights]
    out_shape = [jax.ShapeDtypeStruct((t, w.shape[1]), dt) for w, dt in zip(weights, out_dtypes)]
    return pl.pallas_call(
        functools.partial(_norm_mm_kernel, n_w=n_w, col_chunk=512),
        grid=(t // tm,),
        in_specs=in_specs,
        out_specs=out_specs,
        out_shape=out_shape,
        compiler_params=_cparams(("parallel",)),
        name=name,
    )(x, g.reshape(1, d), scale.reshape(bsz, 1, d), shift.reshape(bsz, 1, d), *weights)


def _kvq_kernel(x_ref, kvg_ref, g_ref, sc_ref, sh_ref, kvw_ref, qw_ref, kv_ref, q_ref, *, col_chunk):
    r = _rms(x_ref[...])
    hk = (r * kvg_ref[...]).astype(BF16)
    hq = ((r * g_ref[...]) * (1.0 + sc_ref[...]) + sh_ref[...]).astype(BF16)
    for hb, w_ref, o_ref in ((hk, kvw_ref, kv_ref), (hq, qw_ref, q_ref)):
        for c0 in range(0, w_ref.shape[1], col_chunk):
            o_ref[:, c0:c0 + col_chunk] = jnp.dot(
                hb, w_ref[:, c0:c0 + col_chunk], preferred_element_type=F32).astype(o_ref.dtype)


def _kvq_proj(x, kv_g, g, scale, shift, kv_w, q_w, seq):
    t, d = x.shape
    bsz = scale.shape[0]
    tm = 512
    row = lambda i: (i, 0)
    const = lambda i: (0, 0)
    per_batch = lambda i: (i * tm // seq, 0, 0)
    return pl.pallas_call(
        functools.partial(_kvq_kernel, col_chunk=512),
        grid=(t // tm,),
        in_specs=[
            pl.BlockSpec((tm, d), row),
            pl.BlockSpec((1, d), const),
            pl.BlockSpec((1, d), const),
            pl.BlockSpec((None, 1, d), per_batch),
            pl.BlockSpec((None, 1, d), per_batch),
            pl.BlockSpec(kv_w.shape, const),
            pl.BlockSpec(q_w.shape, const),
        ],
        out_specs=[pl.BlockSpec((tm, kv_w.shape[1]), row), pl.BlockSpec((tm, q_w.shape[1]), row)],
        out_shape=[jax.ShapeDtypeStruct((t, kv_w.shape[1]), BF16),
                   jax.ShapeDtypeStruct((t, q_w.shape[1]), BF16)],
        compiler_params=_cparams(("parallel",)),
        name="kv_q_proj",
    )(x, kv_g.reshape(1, d), g.reshape(1, d), scale.reshape(bsz, 1, d), shift.reshape(bsz, 1, d),
      kv_w, q_w)


def _mm_resid_kernel(y_ref, w_ref, x_ref, gate_ref, pg_ref, o_ref):
    y = jnp.dot(y_ref[...], w_ref[...], preferred_element_type=F32)
    o_ref[...] = x_ref[...] + gate_ref[...] * (_rms(y) * pg_ref[...])


def _mm_resid(y, w, x, gate, post_g, seq, name):
    t, k = y.shape
    d = w.shape[1]
    bsz = gate.shape[0]
    tm = 512
    return pl.pallas_call(
        _mm_resid_kernel,
        grid=(t // tm,),
        in_specs=[
            pl.BlockSpec((tm, k), lambda i: (i, 0)),
            pl.BlockSpec((k, d), lambda i: (0, 0)),
            pl.BlockSpec((tm, d), lambda i: (i, 0)),
            pl.BlockSpec((None, 1, d), lambda i: (i * tm // seq, 0, 0)),
            pl.BlockSpec((1, d), lambda i: (0, 0)),
        ],
        out_specs=pl.BlockSpec((tm, d), lambda i: (i, 0)),
        out_shape=jax.ShapeDtypeStruct((t, d), F32),
        compiler_params=_cparams(("parallel",)),
        name=name,
    )(y, w, x, gate.reshape(bsz, 1, d), post_g.reshape(1, d))


def _ssd_kernel(z_ref, xbc_ref, dt_ref, cw_ref, cb_ref, dtb_ref, alog_ref, dsk_ref, ng_ref, e_ref,
                y_ref, ext_ref, state_ref, act_ref, yacc_ref):
    L = SSD_CHUNK
    c = pl.program_id(1)

    @pl.when(c == 0)
    def _():
        ext_ref[0:SUBLANES, :] = jnp.zeros((SUBLANES, CONV_DIM), F32)
        state_ref[...] = jnp.zeros(state_ref.shape, F32)

    ext_ref[SUBLANES:SUBLANES + L, :] = xbc_ref[...].astype(F32)
    cc = 512
    for c0 in range(0, CONV_DIM, cc):
        acc = cb_ref[:, c0:c0 + cc] + ext_ref[SUBLANES:SUBLANES + L, c0:c0 + cc] * cw_ref[3:4, c0:c0 + cc]
        for j in range(SSD_CONV - 1):
            r0 = SUBLANES - (SSD_CONV - 1) + j
            acc = acc + ext_ref[r0:r0 + L, c0:c0 + cc] * cw_ref[j:j + 1, c0:c0 + cc]
        act_ref[:, c0:c0 + cc] = acc * _sigmoid(acc)
    ext_ref[0:SUBLANES, :] = ext_ref[L:L + SUBLANES, :]

    row = lax.broadcasted_iota(jnp.int32, (L, L), 0)
    col = lax.broadcasted_iota(jnp.int32, (L, L), 1)
    causal = col <= row
    lane_lo = col < SSD_HEADDIM

    dt = _softplus(dt_ref[...] + dtb_ref[...])
    da = dt * (-jnp.exp(alog_ref[...]))
    ltri = jnp.where(causal, 1.0, 0.0).astype(BF16)
    da_hi = da.astype(BF16)
    da_lo = (da - da_hi.astype(F32)).astype(BF16)
    a_cum = (jnp.dot(ltri, da_hi, preferred_element_type=F32)
             + jnp.dot(ltri, da_lo, preferred_element_type=F32))
    a_cum_t = a_cum.T
    exp_a = jnp.exp(a_cum)
    decay = jnp.exp(a_cum[L - 1:L, :] - a_cum)

    e = e_ref[...]
    dt_x = jnp.dot(dt.astype(BF16), e, preferred_element_type=F32)
    exp_a_x = jnp.dot(exp_a.astype(BF16), e, preferred_element_type=F32)
    decay_x = jnp.dot(decay.astype(BF16), e, preferred_element_type=F32)

    gn = SSD_GROUPS * SSD_STATE
    pairs_per_group = SSD_HEADS // SSD_GROUPS // 2
    for g in range(SSD_GROUPS):
        b_g = act_ref[:, D_INNER + g * SSD_STATE:D_INNER + (g + 1) * SSD_STATE]
        c_g = act_ref[:, D_INNER + gn + g * SSD_STATE:D_INNER + gn + (g + 1) * SSD_STATE].astype(BF16)
        b_gt = b_g.T.astype(BF16)
        cb = jnp.dot(c_g, b_gt, preferred_element_type=F32)
        for pp in range(pairs_per_group):
            p = g * pairs_per_group + pp
            sl = slice(p * LANES, (p + 1) * LANES)
            xs_p = act_ref[:, sl]
            xdt = xs_p * dt_x[:, sl]
            xdt_b = xdt.astype(BF16)
            w_b = (xdt * decay_x[:, sl]).astype(BF16)
            yd = []
            for i in range(2):
                h = 2 * p + i
                seg = a_cum[:, h:h + 1] - a_cum_t[h:h + 1, :]
                lm = jnp.where(causal, jnp.exp(jnp.minimum(seg, 0.0)), 0.0)
                m = (cb * lm).astype(BF16)
                yd.append(jnp.dot(m, xdt_b, preferred_element_type=F32))
            y_diag = jnp.where(lane_lo, yd[0], yd[1])
            prev_t = state_ref[p]
            y_off = jnp.dot(c_g, prev_t.astype(BF16), preferred_element_type=F32) * exp_a_x[:, sl]
            s_t = jnp.dot(b_gt, w_b, preferred_element_type=F32)
            state_ref[p] = prev_t * exp_a_x[L - 1:L, sl] + s_t
            yacc_ref[:, sl] = y_diag + y_off + xs_p * dsk_ref[:, sl]

    gw = D_INNER // SSD_GROUPS
    for g in range(SSD_GROUPS):
        sl = slice(g * gw, (g + 1) * gw)
        zf = z_ref[:, sl].astype(F32)
        gated = yacc_ref[:, sl] * (zf * _sigmoid(zf))
        y_ref[:, sl] = (_rms(gated) * ng_ref[:, sl]).astype(y_ref.dtype)


def _ssd(z, xbc, dt_raw, conv_w, conv_b, dt_bias, a_log, d_skip, norm_g, bsz, seq):
    t = z.shape[0]
    nc = seq // SSD_CHUNK
    L = SSD_CHUNK
    pad = LANES - SSD_HEADS
    dtb = jnp.pad(dt_bias, (0, pad)).reshape(1, LANES)
    alog = jnp.pad(a_log, (0, pad)).reshape(1, LANES)
    dsk = jnp.repeat(d_skip, SSD_HEADDIM).reshape(1, D_INNER)
    expand = (jnp.arange(LANES)[:, None] == (jnp.arange(D_INNER)[None, :] // SSD_HEADDIM)).astype(BF16)
    full = lambda shape: pl.BlockSpec(shape, lambda b, c: (0, 0))
    return pl.pallas_call(
        _ssd_kernel,
        grid=(bsz, nc),
        in_specs=[
            pl.BlockSpec((L, D_INNER), lambda b, c: (b * nc + c, 0)),
            pl.BlockSpec((L, CONV_DIM), lambda b, c: (b * nc + c, 0)),
            pl.BlockSpec((L, LANES), lambda b, c: (b * nc + c, 0)),
            full((SSD_CONV, CONV_DIM)),
            full((1, CONV_DIM)),
            full((1, LANES)),
            full((1, LANES)),
            full((1, D_INNER)),
            full((1, D_INNER)),
            full((LANES, D_INNER)),
        ],
        out_specs=pl.BlockSpec((L, D_INNER), lambda b, c: (b * nc + c, 0)),
        out_shape=jax.ShapeDtypeStruct((t, D_INNER), BF16),
        scratch_shapes=[
            pltpu.VMEM((L + 2 * SUBLANES, CONV_DIM), F32),
            pltpu.VMEM((SSD_HEADS // 2, SSD_STATE, LANES), F32),
            pltpu.VMEM((L, CONV_DIM), F32),
            pltpu.VMEM((L, D_INNER), F32),
        ],
        compiler_params=_cparams(("arbitrary", "arbitrary")),
        name="ssd_scan",
    )(z, xbc, dt_raw, conv_w, conv_b.reshape(1, CONV_DIM), dtb, alog, dsk,
      norm_g.reshape(1, D_INNER), expand)


def _attn_kernel(q_ref, k_ref, v_ref, o_ref, carry_ref, acc_ref):
    bq = q_ref.shape[0]
    qi = pl.program_id(2)
    scale = SB_HEADDIM ** -0.5
    row = lax.broadcasted_iota(jnp.int32, (bq, bq), 0)
    col = lax.broadcasted_iota(jnp.int32, (bq, bq), 1)
    strict = col < row
    tail_sum = jnp.concatenate(
        [jnp.where(row > col, -1.0, 0.0), jnp.full((bq, bq), -1.0, F32)], axis=1).astype(BF16)
    lane_lo = lax.broadcasted_iota(jnp.int32, (bq, LANES), 1) < SB_HEADDIM
    n_pairs = q_ref.shape[1] // LANES
    heads = [(p, i) for p in range(n_pairs) for i in range(2)]

    carry_ref[...] = jnp.zeros(carry_ref.shape, F32)
    acc_ref[...] = jnp.zeros(acc_ref.shape, F32)

    def block(kb, diagonal):
        k0 = pl.multiple_of(kb * bq, bq)
        sls = [slice(p * LANES, (p + 1) * LANES) for p, _ in heads]
        zs = []
        for (p, i), sl in zip(heads, sls):
            q = q_ref[:, sl] * scale
            qm = jnp.where(lane_lo if i == 0 else jnp.logical_not(lane_lo), q, jnp.zeros_like(q))
            zs.append(lax.dot_general(qm, k_ref[pl.ds(k0, bq), sl], (((1,), (1,)), ((), ())),
                                      preferred_element_type=F32))
        sps = [_softplus(z) for z in zs]
        l1ms = [(jnp.where(strict, sp, 0.0) if diagonal else sp).astype(BF16) for sp in sps]
        tss = [jnp.dot(l1m, tail_sum, preferred_element_type=F32) for l1m in l1ms]
        crs = [carry_ref[n] for n in range(len(heads))]
        a_s = []
        for z, sp, ts, cr in zip(zs, sps, tss, crs):
            a = jnp.exp((z - sp) + ts[:, :bq] + cr)
            if diagonal:
                a = jnp.where(strict, a, 0.0)
            a_s.append(a.astype(BF16))
        pvs = [jnp.dot(a, v_ref[pl.ds(k0, bq), sl], preferred_element_type=F32)
               for a, sl in zip(a_s, sls)]
        cmax = None
        for n in range(len(heads)):
            acc_ref[n] += pvs[n]
            cnew = crs[n] + tss[n][:, bq:]
            carry_ref[n] = cnew
            cmax = cnew if cmax is None else jnp.maximum(cmax, cnew)
        return (jnp.max(cmax) > EXP_UNDERFLOW).astype(jnp.int32)

    go0 = block(qi, True)

    def cond(st):
        kb, go = st
        return jnp.logical_and(kb >= 0, go > 0)

    def body(st):
        kb, _ = st
        return kb - 1, block(kb, False)

    lax.while_loop(cond, body, (qi - 1, go0))
    for p in range(n_pairs):
        o_ref[:, p * LANES:(p + 1) * LANES] = jnp.where(
            lane_lo, acc_ref[2 * p], acc_ref[2 * p + 1]).astype(o_ref.dtype)


def _attention(q, kv, bsz, seq):
    t = q.shape[0]
    bq = 128
    nq = seq // bq
    pairs_per_step = 8
    width = pairs_per_step * LANES
    n_groups = D_MODEL // width
    return pl.pallas_call(
        _attn_kernel,
        grid=(bsz, n_groups, nq),
        in_specs=[
            pl.BlockSpec((bq, width), lambda b, p, i: (b * nq + i, p)),
            pl.BlockSpec((seq, width), lambda b, p, i: (b, p)),
            pl.BlockSpec((seq, width), lambda b, p, i: (b, n_groups + p)),
        ],
        out_specs=pl.BlockSpec((bq, width), lambda b, p, i: (b * nq + i, p)),
        out_shape=jax.ShapeDtypeStruct((t, D_MODEL), BF16),
        scratch_shapes=[
            pltpu.VMEM((2 * pairs_per_step, bq, LANES), F32),
            pltpu.VMEM((2 * pairs_per_step, bq, LANES), F32),
        ],
        compiler_params=_cparams(("parallel", "parallel", "arbitrary")),
        name="sb_attention",
    )(q, kv, kv)


def _router_kernel(x_ref, g_ref, sc_ref, sh_ref, rw_ref, rb_ref,
                   h_ref, gate_ref, lpos_ref, tcnt_ref):
    tm = x_ref.shape[0]
    h = _rms(x_ref[...]) * g_ref[...]
    h = h * (1.0 + sc_ref[...]) + sh_ref[...]
    for s in range(ROW_TILES):
        h_ref[pl.ds(s, tm, stride=ROW_TILES), :] = h[:, s * LANES:(s + 1) * LANES]

    w = rw_ref[...]
    h_hi = h.astype(BF16)
    h_lo = (h - h_hi.astype(F32)).astype(BF16)
    w_hi = w.astype(BF16)
    w_lo = (w - w_hi.astype(F32)).astype(BF16)
    logits = (jnp.dot(h_hi, w_hi, preferred_element_type=F32)
              + jnp.dot(h_hi, w_lo, preferred_element_type=F32)
              + jnp.dot(h_lo, w_hi, preferred_element_type=F32)) + rb_ref[...]

    lane = lax.broadcasted_iota(jnp.int32, (tm, LANES), 1).astype(F32)
    work = logits
    vals, idxs = [], []
    chosen = jnp.zeros((tm, LANES), F32)
    for _ in range(TOP_K):
        m = jnp.max(work, axis=1, keepdims=True)
        am = jnp.min(jnp.where(work == m, lane, float(LANES)), axis=1, keepdims=True)
        hit = lane == am
        vals.append(m)
        idxs.append(am)
        chosen = jnp.where(hit, 1.0, chosen)
        work = jnp.where(hit, -jnp.inf, work)
    es = [jnp.exp(v - vals[0]) for v in vals]
    denom = es[0] + es[1] + es[2] + es[3]

    r = lax.broadcasted_iota(jnp.int32, (tm, tm), 0)
    cidx = lax.broadcasted_iota(jnp.int32, (tm, tm), 1)
    lstrict = jnp.where(cidx < r, 1.0, 0.0).astype(BF16)
    before_tile = jnp.dot(lstrict, chosen.astype(BF16), preferred_element_type=F32)

    tile_cnt = jnp.sum(chosen, axis=0, keepdims=True)
    cnt_hi = jnp.floor(tile_cnt * (1.0 / 16.0))
    cnt_lo = tile_cnt - 16.0 * cnt_hi
    er = lax.broadcasted_iota(jnp.int32, (LANES, LANES), 0)
    ec = lax.broadcasted_iota(jnp.int32, (LANES, LANES), 1)
    below = jnp.where(er < ec, 1.0, 0.0).astype(BF16)
    bc = lambda v: jnp.broadcast_to(v, (SUBLANES, LANES)).astype(BF16)
    seg_start = (16.0 * jnp.dot(bc(cnt_hi), below, preferred_element_type=F32)
                 + jnp.dot(bc(cnt_lo), below, preferred_element_type=F32))[0:1, :]
    local = before_tile + seg_start

    gate_out = jnp.zeros((tm, LANES), F32)
    lpos_out = jnp.zeros((tm, LANES), jnp.int32)
    for k in range(TOP_K):
        lp = jnp.sum(jnp.where(lane == idxs[k], local, 0.0), axis=1, keepdims=True)
        gate_out = jnp.where(lane == k, es[k] / denom, gate_out)
        lpos_out = jnp.where(lane == k, lp.astype(jnp.int32), lpos_out)
    gate_ref[...] = gate_out
    lpos_ref[...] = lpos_out
    tcnt_ref[...] = jnp.broadcast_to(tile_cnt, tcnt_ref.shape).astype(jnp.int32)


def _router(x, g, scale, shift, router_w, router_b, seq):
    t, d = x.shape
    bsz = scale.shape[0]
    tm = ROUTE_TM
    pad = LANES - N_EXPERTS
    rw = jnp.pad(router_w, ((0, 0), (0, pad)))
    rb = jnp.pad(router_b, (0, pad), constant_values=-1e30).reshape(1, LANES)
    return pl.pallas_call(
        _router_kernel,
        grid=(t // tm,),
        in_specs=[
            pl.BlockSpec((tm, d), lambda i: (i, 0)),
            pl.BlockSpec((1, d), lambda i: (0, 0)),
            pl.BlockSpec((None, 1, d), lambda i: (i * tm // seq, 0, 0)),
            pl.BlockSpec((None, 1, d), lambda i: (i * tm // seq, 0, 0)),
            pl.BlockSpec((d, LANES), lambda i: (0, 0)),
            pl.BlockSpec((1, LANES), lambda i: (0, 0)),
        ],
        out_specs=[
            pl.BlockSpec((tm * ROW_TILES, LANES), lambda i: (i, 0)),
            pl.BlockSpec((tm, LANES), lambda i: (i, 0)),
            pl.BlockSpec((tm, LANES), lambda i: (i, 0)),
            pl.BlockSpec((SUBLANES, LANES), lambda i: (i, 0)),
        ],
        out_shape=[
            jax.ShapeDtypeStruct((t * ROW_TILES, LANES), F32),
            jax.ShapeDtypeStruct((t, LANES), F32),
            jax.ShapeDtypeStruct((t, LANES), jnp.int32),
            jax.ShapeDtypeStruct((t // tm * SUBLANES, LANES), jnp.int32),
        ],
        compiler_params=_cparams(("parallel",)),
        name="moe_router",
    )(x, g.reshape(1, d), scale.reshape(bsz, 1, d), shift.reshape(bsz, 1, d), rw, rb)


def _segment_copies(tile, cnt_ref, start_ref, base_ref, make_copy):
    top_bit = ROUTE_TM.bit_length() - 1

    def segment(e, carry):
        n = cnt_ref[tile * N_EXPERTS + e]
        local = start_ref[tile * N_EXPERTS + e]
        glob = base_ref[tile * N_EXPERTS + e]
        for b in range(top_bit, -1, -1):
            done = (n >> (b + 1)) << (b + 1)

            @pl.when(((n >> b) & 1) == 1)
            def _():
                make_copy(pl.multiple_of((local + done) * ROW_TILES, ROW_TILES),
                          pl.multiple_of((glob + done) * ROW_TILES, ROW_TILES),
                          (1 << b) * ROW_TILES).start()
        return carry

    lax.fori_loop(0, N_EXPERTS, segment, 0)


def _segment_tables(tile_cnt, off):
    n_tiles = tile_cnt.shape[0]
    zeros_col = jnp.zeros((n_tiles, 1), jnp.int32)
    seg_start = jnp.concatenate([zeros_col, jnp.cumsum(tile_cnt, axis=1)[:, :-1]], axis=1)
    before = jnp.concatenate(
        [jnp.zeros((1, N_EXPERTS), jnp.int32), jnp.cumsum(tile_cnt, axis=0)[:-1]], axis=0)
    base = off[None, :N_EXPERTS] + before
    flat = lambda a: a.astype(jnp.int32).reshape(n_tiles * N_EXPERTS)
    return flat(tile_cnt), flat(seg_start), flat(base)


def _dispatch_kernel(cnt_ref, start_ref, base_ref, lpos_ref, h_ref, xs_ref, loc_ref, sems):
    tm = h_ref.shape[0] // ROW_TILES
    i = pl.program_id(0)
    slot = i % 2
    slot_rows = tm * TOP_K * ROW_TILES
    unroll = 4

    def wait_slot(s):
        pltpu.make_async_copy(loc_ref.at[s], xs_ref.at[pl.ds(0, slot_rows)], sems.at[s]).wait()

    def place(j, carry):
        for u in range(unroll):
            t = j * unroll + u
            v = h_ref[pl.ds(pl.multiple_of(t * ROW_TILES, ROW_TILES), ROW_TILES), :]
            for k in range(TOP_K):
                row = pl.multiple_of(lpos_ref[t * TOP_K + k], ROW_TILES)
                loc_ref[slot, pl.ds(row, ROW_TILES), :] = v
        return carry

    lax.fori_loop(0, tm // unroll, place, 0)

    _segment_copies(i, cnt_ref, start_ref, base_ref,
                    lambda loc_row, glob_row, size: pltpu.make_async_copy(
                        loc_ref.at[slot, pl.ds(loc_row, size)], xs_ref.at[pl.ds(glob_row, size)],
                        sems.at[slot]))

    @pl.when(i > 0)
    def _():
        wait_slot(1 - slot)

    @pl.when(i == pl.num_programs(0) - 1)
    def _():
        wait_slot(slot)


def _dispatch(h_tiles, lpos_flat, tables):
    rows = h_tiles.shape[0]
    t = rows // ROW_TILES
    tm = ROUTE_TM
    n_tiles = t // tm
    grid_spec = pltpu.PrefetchScalarGridSpec(
        num_scalar_prefetch=3,
        grid=(n_tiles,),
        in_specs=[
            pl.BlockSpec((tm * TOP_K,), lambda i, c, s, b: (i,), memory_space=pltpu.SMEM),
            pl.BlockSpec((tm * ROW_TILES, LANES), lambda i, c, s, b: (i, 0)),
        ],
        out_specs=pl.BlockSpec(memory_space=pl.ANY),
        scratch_shapes=[
            pltpu.VMEM((2, tm * TOP_K * ROW_TILES, LANES), F32),
            pltpu.SemaphoreType.DMA((2,)),
        ],
    )
    return pl.pallas_call(
        _dispatch_kernel,
        grid_spec=grid_spec,
        out_shape=jax.ShapeDtypeStruct((rows * TOP_K, LANES), F32),
        compiler_params=_cparams(("arbitrary",)),
        name="moe_dispatch",
    )(*tables, lpos_flat, h_tiles)


def _ffn_kernel(tile_ref, exp_ref, lo_ref, hi_ref, first_ref, newexp_ref,
                xs_ref, win_ref, bg_ref, bu_ref, wout_ref, bo_ref, perm_ref, ys_ref,
                wg_ref, wu_ref, wo_ref):
    del tile_ref, exp_ref
    i = pl.program_id(0)
    tm = xs_ref.shape[0] // ROW_TILES
    lo, hi = lo_ref[i], hi_ref[i]

    @pl.when(newexp_ref[i] == 1)
    def _():
        pw = 2 * LANES
        for cb in range(win_ref.shape[1] // pw):
            blk = win_ref[:, cb * pw:(cb + 1) * pw].astype(BF16)
            res = jnp.dot(blk, perm_ref[...], preferred_element_type=F32).astype(BF16)
            wg_ref[:, cb * LANES:(cb + 1) * LANES] = res[:, :LANES]
            wu_ref[:, cb * LANES:(cb + 1) * LANES] = res[:, LANES:]
        wo_ref[...] = wout_ref[...].astype(BF16)

    def compute():
        x = jnp.concatenate(
            [xs_ref[pl.ds(s, tm, stride=ROW_TILES), :] for s in range(ROW_TILES)], axis=1).astype(BF16)
        gate = jnp.dot(x, wg_ref[...], preferred_element_type=F32) + bg_ref[...]
        up = jnp.dot(x, wu_ref[...], preferred_element_type=F32) + bu_ref[...]
        gate = jnp.minimum(gate, SWIGLU_LIMIT)
        up = jnp.clip(up, -SWIGLU_LIMIT, SWIGLU_LIMIT)
        act = (up + 1.0) * gate * _sigmoid(SWIGLU_ALPHA * gate)
        y = jnp.dot(act.astype(BF16), wo_ref[...], preferred_element_type=F32) + bo_ref[...]
        r = lax.broadcasted_iota(jnp.int32, (tm, 1), 0)
        return jnp.where(jnp.logical_and(r >= lo, r < hi), y, 0.0)

    @pl.when(first_ref[i] == 1)
    def _():
        y = compute()
        for s in range(ROW_TILES):
            ys_ref[pl.ds(s, tm, stride=ROW_TILES), :] = y[:, s * LANES:(s + 1) * LANES]

    @pl.when(jnp.logical_and(first_ref[i] == 0, hi > lo))
    def _():
        y = compute()
        for s in range(ROW_TILES):
            ys_ref[pl.ds(s, tm, stride=ROW_TILES), :] += y[:, s * LANES:(s + 1) * LANES]


def _ffn(xs_tiles, items, w_in, bg, bu, w_out, bo, layer, tm):
    rows = xs_tiles.shape[0]
    n_items = items[0].shape[0]
    d, f2 = w_in.shape[2], w_in.shape[3]
    f = f2 // 2
    pw = 2 * LANES
    src = jnp.arange(pw)[:, None]
    dst = jnp.arange(pw)[None, :]
    perm = (src == jnp.where(dst < LANES, 2 * dst, 2 * (dst - LANES) + 1)).astype(BF16)
    wspec = lambda shape: pl.BlockSpec(
        shape, lambda i, tile, ex, lo, hi, fi, ne: (layer, ex[i], 0, 0))
    tspec = pl.BlockSpec((tm * ROW_TILES, LANES), lambda i, tile, ex, lo, hi, fi, ne: (tile[i], 0))
    grid_spec = pltpu.PrefetchScalarGridSpec(
        num_scalar_prefetch=6,
        grid=(n_items,),
        in_specs=[
            tspec,
            wspec((None, None, d, f2)),
            wspec((None, None, 1, f)),
            wspec((None, None, 1, f)),
            wspec((None, None, f, d)),
            wspec((None, None, 1, d)),
            pl.BlockSpec((pw, pw), lambda i, tile, ex, lo, hi, fi, ne: (0, 0)),
        ],
        out_specs=tspec,
        scratch_shapes=[
            pltpu.VMEM((d, f), BF16),
            pltpu.VMEM((d, f), BF16),
            pltpu.VMEM((f, d), BF16),
        ],
    )
    return pl.pallas_call(
        _ffn_kernel,
        grid_spec=grid_spec,
        out_shape=jax.ShapeDtypeStruct((rows, LANES), F32),
        compiler_params=_cparams(("arbitrary",)),
        name="moe_ffn",
    )(*items, xs_tiles, w_in, bg, bu, w_out, bo, perm)


def _combine_kernel(cnt_ref, start_ref, base_ref, lpos_ref, gates_ref, x_ref, gf_ref, pg_ref,
                    ys_ref, o_ref, loc_ref, moe_ref, sems):
    tm = x_ref.shape[0]
    i = pl.program_id(0)
    slot = i % 2
    slot_rows = tm * TOP_K * ROW_TILES
    unroll = 4

    def fetch(tile, s):
        _segment_copies(tile, cnt_ref, start_ref, base_ref,
                        lambda loc_row, glob_row, size: pltpu.make_async_copy(
                            ys_ref.at[pl.ds(glob_row, size)], loc_ref.at[s, pl.ds(loc_row, size)],
                            sems.at[s]))

    @pl.when(i == 0)
    def _():
        fetch(i, 0)

    @pl.when(i + 1 < pl.num_programs(0))
    def _():
        fetch(i + 1, 1 - slot)

    pltpu.make_async_copy(ys_ref.at[pl.ds(0, slot_rows)], loc_ref.at[slot], sems.at[slot]).wait()

    def mix(j, carry):
        for u in range(unroll):
            t = j * unroll + u
            acc = None
            for k in range(TOP_K):
                row = pl.multiple_of(lpos_ref[t * TOP_K + k], ROW_TILES)
                term = gates_ref[t * TOP_K + k] * loc_ref[slot, pl.ds(row, ROW_TILES), :]
                acc = term if acc is None else acc + term
            moe_ref[pl.ds(pl.multiple_of(t * ROW_TILES, ROW_TILES), ROW_TILES), :] = acc
        return carry

    lax.fori_loop(0, tm // unroll, mix, 0)

    moe = jnp.concatenate(
        [moe_ref[pl.ds(s, tm, stride=ROW_TILES), :] for s in range(ROW_TILES)], axis=1)
    o_ref[...] = x_ref[...] + gf_ref[...] * (_rms(moe) * pg_ref[...])


def _combine(ys_tiles, lpos_flat, gates_flat, tables, x, gate_f, post_g, seq):
    t, d = x.shape
    bsz = gate_f.shape[0]
    tm = ROUTE_TM
    grid_spec = pltpu.PrefetchScalarGridSpec(
        num_scalar_prefetch=3,
        grid=(t // tm,),
        in_specs=[
            pl.BlockSpec((tm * TOP_K,), lambda i, c, s, b: (i,), memory_space=pltpu.SMEM),
            pl.BlockSpec((tm * TOP_K,), lambda i, c, s, b: (i,), memory_space=pltpu.SMEM),
            pl.BlockSpec((tm, d), lambda i, c, s, b: (i, 0)),
            pl.BlockSpec((None, 1, d), lambda i, c, s, b: (i * tm // seq, 0, 0)),
            pl.BlockSpec((1, d), lambda i, c, s, b: (0, 0)),
            pl.BlockSpec(memory_space=pl.ANY),
        ],
        out_specs=pl.BlockSpec((tm, d), lambda i, c, s, b: (i, 0)),
        scratch_shapes=[
            pltpu.VMEM((2, tm * TOP_K * ROW_TILES, LANES), F32),
            pltpu.VMEM((tm * ROW_TILES, LANES), F32),
            pltpu.SemaphoreType.DMA((2,)),
        ],
    )
    return pl.pallas_call(
        _combine_kernel,
        grid_spec=grid_spec,
        out_shape=jax.ShapeDtypeStruct((t, d), F32),
        compiler_params=_cparams(("arbitrary",)),
        name="moe_combine",
    )(*tables, lpos_flat, gates_flat, x, gate_f.reshape(bsz, 1, d), post_g.reshape(1, d), ys_tiles)


def _ffn_items(counts, n_rows, tm):
    n_tiles = n_rows // tm
    n_items = n_tiles + N_EXPERTS - 1
    off = jnp.concatenate([jnp.zeros((1,), jnp.int32), jnp.cumsum(counts).astype(jnp.int32)])
    first_tile = off[:-1] // tm
    last_tile = (off[1:] - 1) // tm
    per = jnp.where(counts > 0, last_tile - first_tile + 1, 0)
    istart = jnp.concatenate([jnp.zeros((1,), jnp.int32), jnp.cumsum(per).astype(jnp.int32)])
    total = istart[-1]
    i = jnp.arange(n_items, dtype=jnp.int32)
    e = jnp.sum((istart[None, :] <= i[:, None]).astype(jnp.int32), axis=1) - 1
    e = jnp.clip(e, 0, N_EXPERTS - 1)
    valid = i < total
    e = jnp.where(valid, e, e[jnp.maximum(total - 1, 0)])
    tile = jnp.where(valid, first_tile[e] + (i - istart[e]), n_tiles - 1)
    lo = jnp.where(valid, jnp.clip(off[e] - tile * tm, 0, tm), 0)
    hi = jnp.where(valid, jnp.clip(off[e + 1] - tile * tm, 0, tm), 0)
    prev_tile = jnp.concatenate([jnp.full((1,), -1, jnp.int32), tile[:-1]])
    first = (tile != prev_tile).astype(jnp.int32)
    prev_e = jnp.concatenate([jnp.full((1,), -1, jnp.int32), e[:-1]])
    new_expert = (e != prev_e).astype(jnp.int32)
    return off, (tile.astype(jnp.int32), e.astype(jnp.int32), lo.astype(jnp.int32),
                 hi.astype(jnp.int32), first, new_expert)


def _moe_layer(x, pre_g, scale, shift, gate_f, post_g, router_w, router_b,
               w_in, b_in, w_out, b_out, layer, seq):
    t = x.shape[0]
    depth = w_in.shape[0]
    ffn_tm = 512
    h_tiles, gates, lpos, tcnt = _router(x, pre_g, scale, shift, router_w, router_b, seq)
    tile_cnt = tcnt.reshape(t // ROUTE_TM, SUBLANES, LANES)[:, 0, :N_EXPERTS]
    counts = jnp.sum(tile_cnt, axis=0)
    off, items = _ffn_items(counts, t * TOP_K, ffn_tm)
    tables = _segment_tables(tile_cnt, off)
    lpos_flat = (lpos[:, :TOP_K] * ROW_TILES).reshape(t * TOP_K)
    xs = _dispatch(h_tiles, lpos_flat, tables)
    bg = b_in[:, :, 0::2].reshape(depth, N_EXPERTS, 1, -1)
    bu = b_in[:, :, 1::2].reshape(depth, N_EXPERTS, 1, -1)
    ys = _ffn(xs, items, w_in, bg, bu, w_out, b_out.reshape(depth, N_EXPERTS, 1, -1), layer, ffn_tm)
    return _combine(ys, lpos_flat, gates[:, :TOP_K].reshape(t * TOP_K), tables, x, gate_f, post_g, seq)


def kernel(x, c, ada_w, ada_b, mix_pre_g, mix_post_g, ffn_pre_g, ffn_post_g, ssd_in_w, ssd_conv_w,
           ssd_conv_b, ssd_dt_bias, ssd_a_log, ssd_d, ssd_norm_g, ssd_out_w, kv_norm_g, kv_w, sb_q_w,
           sb_o_w, router_w, router_b, exp_w_in, exp_b_in, exp_w_out, exp_b_out):
    bsz, seq, d = x.shape
    t = bsz * seq
    xf = x.reshape(t, d)
    mod = _ada_mod(c, ada_w, ada_b)

    sh_m, sc_m, g_m, sh_f, sc_f, g_f = [mod[0, :, k * d:(k + 1) * d] for k in range(6)]
    in_w = ssd_in_w[0]
    w_z = in_w[:, :D_INNER].astype(BF16)
    w_xbc = in_w[:, D_INNER:D_INNER + CONV_DIM].astype(BF16)
    w_dt = jnp.pad(in_w[:, D_INNER + CONV_DIM:], ((0, 0), (0, LANES - SSD_HEADS))).astype(BF16)
    z, xbc, dt_raw = _norm_mm(xf, mix_pre_g[0], sc_m, sh_m, [w_z, w_xbc, w_dt],
                              [BF16, BF16, F32], seq, "ssd_in_proj")
    y = _ssd(z, xbc, dt_raw, ssd_conv_w[0], ssd_conv_b[0], ssd_dt_bias[0], ssd_a_log[0],
             ssd_d[0], ssd_norm_g[0], bsz, seq)
    xf = _mm_resid(y, ssd_out_w[0].astype(BF16), xf, g_m, mix_post_g[0], seq, "ssd_out_proj")
    xf = _moe_layer(xf, ffn_pre_g[0], sc_f, sh_f, g_f, ffn_post_g[0], router_w[0], router_b[0],
                    exp_w_in, exp_b_in, exp_w_out, exp_b_out, 0, seq)

    sh_m, sc_m, g_m, sh_f, sc_f, g_f = [mod[1, :, k * d:(k + 1) * d] for k in range(6)]
    kv, q = _kvq_proj(xf, kv_norm_g, mix_pre_g[1], sc_m, sh_m, kv_w.astype(BF16),
                      sb_q_w[0].astype(BF16), seq)
    att = _attention(q, kv, bsz, seq)
    xf = _mm_resid(att, sb_o_w[0].astype(BF16), xf, g_m, mix_post_g[1], seq, "attn_out_proj")
    xf = _moe_layer(xf, ffn_pre_g[1], sc_f, sh_f, g_f, ffn_post_g[1], router_w[1], router_b[1],
                    exp_w_in, exp_b_in, exp_w_out, exp_b_out, 1, seq)
    return xf.reshape(bsz, seq, d)
```

```python
import functools

import jax
import jax.numpy as jnp
from jax import lax
from jax.experimental import pallas as pl
from jax.experimental.pallas import tpu as pltpu

F32 = jnp.float32
BF16 = jnp.bfloat16

D_MODEL = 1024
D_INNER = 2048
SSD_HEADDIM = 64
SSD_HEADS = 32
SSD_GROUPS = 4
SSD_STATE = 128
SSD_CONV = 4
SSD_CHUNK = 128
CONV_DIM = D_INNER + 2 * SSD_GROUPS * SSD_STATE
SB_HEADS = 16
SB_HEADDIM = 64
N_EXPERTS = 32
TOP_K = 4
SWIGLU_LIMIT = 7.0
SWIGLU_ALPHA = 1.702
NORM_EPS = 1e-6

LANES = 128
SUBLANES = 8
ROW_TILES = D_MODEL // LANES
VMEM_LIMIT = 56 * 1024 * 1024

EXP_UNDERFLOW = -88.0

ROUTE_TM = 512


def _cparams(sem):
    return pltpu.CompilerParams(dimension_semantics=sem, vmem_limit_bytes=VMEM_LIMIT)


def _softplus(x):
    return jnp.maximum(x, 0.0) + jnp.log(1.0 + jnp.exp(-jnp.abs(x)))


def _sigmoid(x):
    return 0.5 * jnp.tanh(0.5 * x) + 0.5


def _rms(x):
    return x * lax.rsqrt(jnp.mean(x * x, axis=-1, keepdims=True) + NORM_EPS)


def _ada_kernel(c_ref, w_ref, b_ref, o_ref):
    c = c_ref[...]
    ca = (c * _sigmoid(c)).astype(BF16)
    o_ref[...] = jnp.dot(ca, w_ref[...].astype(BF16), preferred_element_type=F32) + b_ref[...]


def _ada_mod(c, ada_w, ada_b):
    depth, d, n = ada_w.shape
    bsz = c.shape[0]
    tn = 1024
    return pl.pallas_call(
        _ada_kernel,
        grid=(depth, n // tn),
        in_specs=[
            pl.BlockSpec((bsz, d), lambda l, j: (0, 0)),
            pl.BlockSpec((None, d, tn), lambda l, j: (l, 0, j)),
            pl.BlockSpec((None, 1, tn), lambda l, j: (l, 0, j)),
        ],
        out_specs=pl.BlockSpec((None, bsz, tn), lambda l, j: (l, 0, j)),
        out_shape=jax.ShapeDtypeStruct((depth, bsz, n), F32),
        compiler_params=_cparams(("parallel", "parallel")),
        name="ada_mod",
    )(c, ada_w, ada_b.reshape(depth, 1, n))


def _norm_mm_kernel(x_ref, g_ref, sc_ref, sh_ref, *refs, n_w, col_chunk):
    w_refs, o_refs = refs[:n_w], refs[n_w:]
    h = _rms(x_ref[...]) * g_ref[...]
    h = h * (1.0 + sc_ref[...]) + sh_ref[...]
    hb = h.astype(BF16)
    for w_ref, o_ref in zip(w_refs, o_refs):
        n = w_ref.shape[1]
        for c0 in range(0, n, col_chunk):
            cw = min(col_chunk, n - c0)
            o_ref[:, c0:c0 + cw] = jnp.dot(
                hb, w_ref[:, c0:c0 + cw], preferred_element_type=F32).astype(o_ref.dtype)


def _norm_mm(x, g, scale, shift, weights, out_dtypes, seq, name):
    t, d = x.shape
    bsz = scale.shape[0]
    tm = 256
    n_w = len(weights)
    in_specs = [
        pl.BlockSpec((tm, d), lambda i: (i, 0)),
        pl.BlockSpec((1, d), lambda i: (0, 0)),
        pl.BlockSpec((None, 1, d), lambda i: (i * tm // seq, 0, 0)),
        pl.BlockSpec((None, 1, d), lambda i: (i * tm // seq, 0, 0)),
    ] + [pl.BlockSpec(w.shape, lambda i: (0, 0)) for w in weights]
    out_specs = [pl.BlockSpec((tm, w.shape[1]), lambda i: (i, 0)) for w in weights]
    out_shape = [jax.ShapeDtypeStruct((t, w.shape[1]), dt) for w, dt in zip(weights, out_dtypes)]
    return pl.pallas_call(
        functools.partial(_norm_mm_kernel, n_w=n_w, col_chunk=512),
        grid=(t // tm,),
        in_specs=in_specs,
        out_specs=out_specs,
        out_shape=out_shape,
        compiler_params=_cparams(("parallel",)),
        name=name,
    )(x, g.reshape(1, d), scale.reshape(bsz, 1, d), shift.reshape(bsz, 1, d), *weights)


def _kvq_kernel(x_ref, kvg_ref, g_ref, sc_ref, sh_ref, kvw_ref, qw_ref, kv_ref, q_ref, *, col_chunk):
    r = _rms(x_ref[...])
    hk = (r * kvg_ref[...]).astype(BF16)
    hq = ((r * g_ref[...]) * (1.0 + sc_ref[...]) + sh_ref[...]).astype(BF16)
    for hb, w_ref, o_ref in ((hk, kvw_ref, kv_ref), (hq, qw_ref, q_ref)):
        for c0 in range(0, w_ref.shape[1], col_chunk):
            o_ref[:, c0:c0 + col_chunk] = jnp.dot(
                hb, w_ref[:, c0:c0 + col_chunk], preferred_element_type=F32).astype(o_ref.dtype)


def _kvq_proj(x, kv_g, g, scale, shift, kv_w, q_w, seq):
    t, d = x.shape
    bsz = scale.shape[0]
    tm = 512
    row = lambda i: (i, 0)
    const = lambda i: (0, 0)
    per_batch = lambda i: (i * tm // seq, 0, 0)
    return pl.pallas_call(
        functools.partial(_kvq_kernel, col_chunk=512),
        grid=(t // tm,),
        in_specs=[
            pl.BlockSpec((tm, d), row),
            pl.BlockSpec((1, d), const),
            pl.BlockSpec((1, d), const),
            pl.BlockSpec((None, 1, d), per_batch),
            pl.BlockSpec((None, 1, d), per_batch),
            pl.BlockSpec(kv_w.shape, const),
            pl.BlockSpec(q_w.shape, const),
        ],
        out_specs=[pl.BlockSpec((tm, kv_w.shape[1]), row), pl.BlockSpec((tm, q_w.shape[1]), row)],
        out_shape=[jax.ShapeDtypeStruct((t, kv_w.shape[1]), BF16),
                   jax.ShapeDtypeStruct((t, q_w.shape[1]), BF16)],
        compiler_params=_cparams(("parallel",)),
        name="kv_q_proj",
    )(x, kv_g.reshape(1, d), g.reshape(1, d), scale.reshape(bsz, 1, d), shift.reshape(bsz, 1, d),
      kv_w, q_w)


def _mm_resid_kernel(y_ref, w_ref, x_ref, gate_ref, pg_ref, o_ref):
    y = jnp.dot(y_ref[...], w_ref[...], preferred_element_type=F32)
    o_ref[...] = x_ref[...] + gate_ref[...] * (_rms(y) * pg_ref[...])


def _mm_resid(y, w, x, gate, post_g, seq, name):
    t, k = y.shape
    d = w.shape[1]
    bsz = gate.shape[0]
    tm = 512
    return pl.pallas_call(
        _mm_resid_kernel,
        grid=(t // tm,),
        in_specs=[
            pl.BlockSpec((tm, k), lambda i: (i, 0)),
            pl.BlockSpec((k, d), lambda i: (0, 0)),
            pl.BlockSpec((tm, d), lambda i: (i, 0)),
            pl.BlockSpec((None, 1, d), lambda i: (i * tm // seq, 0, 0)),
            pl.BlockSpec((1, d), lambda i: (0, 0)),
        ],
        out_specs=pl.BlockSpec((tm, d), lambda i: (i, 0)),
        out_shape=jax.ShapeDtypeStruct((t, d), F32),
        compiler_params=_cparams(("parallel",)),
        name=name,
    )(y, w, x, gate.reshape(bsz, 1, d), post_g.reshape(1, d))


def _ssd_kernel(z_ref, xbc_ref, dt_ref, cw_ref, cb_ref, dtb_ref, alog_ref, dsk_ref, ng_ref, e_ref,
                y_ref, ext_ref, state_ref, act_ref, yacc_ref):
    L = SSD_CHUNK
    c = pl.program_id(1)

    @pl.when(c == 0)
    def _():
        ext_ref[0:SUBLANES, :] = jnp.zeros((SUBLANES, CONV_DIM), F32)
        state_ref[...] = jnp.zeros(state_ref.shape, F32)

    ext_ref[SUBLANES:SUBLANES + L, :] = xbc_ref[...].astype(F32)
    cc = 512
    for c0 in range(0, CONV_DIM, cc):
        acc = cb_ref[:, c0:c0 + cc] + ext_ref[SUBLANES:SUBLANES + L, c0:c0 + cc] * cw_ref[3:4, c0:c0 + cc]
        for j in range(SSD_CONV - 1):
            r0 = SUBLANES - (SSD_CONV - 1) + j
            acc = acc + ext_ref[r0:r0 + L, c0:c0 + cc] * cw_ref[j:j + 1, c0:c0 + cc]
        act_ref[:, c0:c0 + cc] = acc * _sigmoid(acc)
    ext_ref[0:SUBLANES, :] = ext_ref[L:L + SUBLANES, :]

    row = lax.broadcasted_iota(jnp.int32, (L, L), 0)
    col = lax.broadcasted_iota(jnp.int32, (L, L), 1)
    causal = col <= row
    lane_lo = col < SSD_HEADDIM

    dt = _softplus(dt_ref[...] + dtb_ref[...])
    da = dt * (-jnp.exp(alog_ref[...]))
    ltri = jnp.where(causal, 1.0, 0.0).astype(BF16)
    da_hi = da.astype(BF16)
    da_lo = (da - da_hi.astype(F32)).astype(BF16)
    a_cum = (jnp.dot(ltri, da_hi, preferred_element_type=F32)
             + jnp.dot(ltri, da_lo, preferred_element_type=F32))
    a_cum_t = a_cum.T
    exp_a = jnp.exp(a_cum)
    decay = jnp.exp(a_cum[L - 1:L, :] - a_cum)

    e = e_ref[...]
    dt_x = jnp.dot(dt.astype(BF16), e, preferred_element_type=F32)
    exp_a_x = jnp.dot(exp_a.astype(BF16), e, preferred_element_type=F32)
    decay_x = jnp.dot(decay.astype(BF16), e, preferred_element_type=F32)

    gn = SSD_GROUPS * SSD_STATE
    pairs_per_group = SSD_HEADS // SSD_GROUPS // 2
    for g in range(SSD_GROUPS):
        b_g = act_ref[:, D_INNER + g * SSD_STATE:D_INNER + (g + 1) * SSD_STATE]
        c_g = act_ref[:, D_INNER + gn + g * SSD_STATE:D_INNER + gn + (g + 1) * SSD_STATE].astype(BF16)
        b_gt = b_g.T.astype(BF16)
        cb = jnp.dot(c_g, b_gt, preferred_element_type=F32)
        for pp in range(pairs_per_group):
            p = g * pairs_per_group + pp
            sl = slice(p * LANES, (p + 1) * LANES)
            xs_p = act_ref[:, sl]
            xdt = xs_p * dt_x[:, sl]
            xdt_b = xdt.astype(BF16)
            w_b = (xdt * decay_x[:, sl]).astype(BF16)
            yd = []
            for i in range(2):
                h = 2 * p + i
                seg = a_cum[:, h:h + 1] - a_cum_t[h:h + 1, :]
                lm = jnp.where(causal, jnp.exp(jnp.minimum(seg, 0.0)), 0.0)
                m = (cb * lm).astype(BF16)
                yd.append(jnp.dot(m, xdt_b, preferred_element_type=F32))
            y_diag = jnp.where(lane_lo, yd[0], yd[1])
            prev_t = state_ref[p]
            y_off = jnp.dot(c_g, prev_t.astype(BF16), preferred_element_type=F32) * exp_a_x[:, sl]
            s_t = jnp.dot(b_gt, w_b, preferred_element_type=F32)
            state_ref[p] = prev_t * exp_a_x[L - 1:L, sl] + s_t
            yacc_ref[:, sl] = y_diag + y_off + xs_p * dsk_ref[:, sl]

    gw = D_INNER // SSD_GROUPS
    for g in range(SSD_GROUPS):
        sl = slice(g * gw, (g + 1) * gw)
        zf = z_ref[:, sl].astype(F32)
        gated = yacc_ref[:, sl] * (zf * _sigmoid(zf))
        y_ref[:, sl] = (_rms(gated) * ng_ref[:, sl]).astype(y_ref.dtype)


def _ssd(z, xbc, dt_raw, conv_w, conv_b, dt_bias, a_log, d_skip, norm_g, bsz, seq):
    t = z.shape[0]
    nc = seq // SSD_CHUNK
    L = SSD_CHUNK
    pad = LANES - SSD_HEADS
    dtb = jnp.pad(dt_bias, (0, pad)).reshape(1, LANES)
    alog = jnp.pad(a_log, (0, pad)).reshape(1, LANES)
    dsk = jnp.repeat(d_skip, SSD_HEADDIM).reshape(1, D_INNER)
    expand = (jnp.arange(LANES)[:, None] == (jnp.arange(D_INNER)[None, :] // SSD_HEADDIM)).astype(BF16)
    full = lambda shape: pl.BlockSpec(shape, lambda b, c: (0, 0))
    return pl.pallas_call(
        _ssd_kernel,
        grid=(bsz, nc),
        in_specs=[
            pl.BlockSpec((L, D_INNER), lambda b, c: (b * nc + c, 0)),
            pl.BlockSpec((L, CONV_DIM), lambda b, c: (b * nc + c, 0)),
            pl.BlockSpec((L, LANES), lambda b, c: (b * nc + c, 0)),
            full((SSD_CONV, CONV_DIM)),
            full((1, CONV_DIM)),
            full((1, LANES)),
            full((1, LANES)),
            full((1, D_INNER)),
            full((1, D_INNER)),
            full((LANES, D_INNER)),
        ],
        out_specs=pl.BlockSpec((L, D_INNER), lambda b, c: (b * nc + c, 0)),
        out_shape=jax.ShapeDtypeStruct((t, D_INNER), BF16),
        scratch_shapes=[
            pltpu.VMEM((L + 2 * SUBLANES, CONV_DIM), F32),
            pltpu.VMEM((SSD_HEADS // 2, SSD_STATE, LANES), F32),
            pltpu.VMEM((L, CONV_DIM), F32),
            pltpu.VMEM((L, D_INNER), F32),
        ],
        compiler_params=_cparams(("arbitrary", "arbitrary")),
        name="ssd_scan",
    )(z, xbc, dt_raw, conv_w, conv_b.reshape(1, CONV_DIM), dtb, alog, dsk,
      norm_g.reshape(1, D_INNER), expand)


def _attn_kernel(q_ref, k_ref, v_ref, o_ref, carry_ref, acc_ref):
    bq = q_ref.shape[0]
    qi = pl.program_id(2)
    scale = SB_HEADDIM ** -0.5
    row = lax.broadcasted_iota(jnp.int32, (bq, bq), 0)
    col = lax.broadcasted_iota(jnp.int32, (bq, bq), 1)
    strict = col < row
    tail_sum = jnp.concatenate(
        [jnp.where(row > col, -1.0, 0.0), jnp.full((bq, bq), -1.0, F32)], axis=1).astype(BF16)
    lane_lo = lax.broadcasted_iota(jnp.int32, (bq, LANES), 1) < SB_HEADDIM
    n_pairs = q_ref.shape[1] // LANES
    heads = [(p, i) for p in range(n_pairs) for i in range(2)]

    carry_ref[...] = jnp.zeros(carry_ref.shape, F32)
    acc_ref[...] = jnp.zeros(acc_ref.shape, F32)

    def block(kb, diagonal):
        k0 = pl.multiple_of(kb * bq, bq)
        sls = [slice(p * LANES, (p + 1) * LANES) for p, _ in heads]
        zs = []
        for (p, i), sl in zip(heads, sls):
            q = q_ref[:, sl] * scale
            qm = jnp.where(lane_lo if i == 0 else jnp.logical_not(lane_lo), q, jnp.zeros_like(q))
            zs.append(lax.dot_general(qm, k_ref[pl.ds(k0, bq), sl], (((1,), (1,)), ((), ())),
                                      preferred_element_type=F32))
        sps = [_softplus(z) for z in zs]
        l1ms = [(jnp.where(strict, sp, 0.0) if diagonal else sp).astype(BF16) for sp in sps]
        tss = [jnp.dot(l1m, tail_sum, preferred_element_type=F32) for l1m in l1ms]
        crs = [carry_ref[n] for n in range(len(heads))]
        a_s = []
        for z, sp, ts, cr in zip(zs, sps, tss, crs):
            a = jnp.exp((z - sp) + ts[:, :bq] + cr)
            if diagonal:
                a = jnp.where(strict, a, 0.0)
            a_s.append(a.astype(BF16))
        pvs = [jnp.dot(a, v_ref[pl.ds(k0, bq), sl], preferred_element_type=F32)
               for a, sl in zip(a_s, sls)]
        cmax = None
        for n in range(len(heads)):
            acc_ref[n] += pvs[n]
            cnew = crs[n] + tss[n][:, bq:]
            carry_ref[n] = cnew
            cmax = cnew if cmax is None else jnp.maximum(cmax, cnew)
        return (jnp.max(cmax) > EXP_UNDERFLOW).astype(jnp.int32)

    go0 = block(qi, True)

    def cond(st):
        kb, go = st
        return jnp.logical_and(kb >= 0, go > 0)

    def body(st):
        kb, _ = st
        return kb - 1, block(kb, False)

    lax.while_loop(cond, body, (qi - 1, go0))
    for p in range(n_pairs):
        o_ref[:, p * LANES:(p + 1) * LANES] = jnp.where(
            lane_lo, acc_ref[2 * p], acc_ref[2 * p + 1]).astype(o_ref.dtype)


def _attention(q, kv, bsz, seq):
    t = q.shape[0]
    bq = 128
    nq = seq // bq
    pairs_per_step = 8
    width = pairs_per_step * LANES
    n_groups = D_MODEL // width
    return pl.pallas_call(
        _attn_kernel,
        grid=(bsz, n_groups, nq),
        in_specs=[
            pl.BlockSpec((bq, width), lambda b, p, i: (b * nq + i, p)),
            pl.BlockSpec((seq, width), lambda b, p, i: (b, p)),
            pl.BlockSpec((seq, width), lambda b, p, i: (b, n_groups + p)),
        ],
        out_specs=pl.BlockSpec((bq, width), lambda b, p, i: (b * nq + i, p)),
        out_shape=jax.ShapeDtypeStruct((t, D_MODEL), BF16),
        scratch_shapes=[
            pltpu.VMEM((2 * pairs_per_step, bq, LANES), F32),
            pltpu.VMEM((2 * pairs_per_step, bq, LANES), F32),
        ],
        compiler_params=_cparams(("parallel", "parallel", "arbitrary")),
        name="sb_attention",
    )(q, kv, kv)


def _router_kernel(x_ref, g_ref, sc_ref, sh_ref, rw_ref, rb_ref,
                   h_ref, gate_ref, lpos_ref, tcnt_ref):
    tm = x_ref.shape[0]
    h = _rms(x_ref[...]) * g_ref[...]
    h = h * (1.0 + sc_ref[...]) + sh_ref[...]
    for s in range(ROW_TILES):
        h_ref[pl.ds(s, tm, stride=ROW_TILES), :] = h[:, s * LANES:(s + 1) * LANES]

    w = rw_ref[...]
    h_hi = h.astype(BF16)
    h_lo = (h - h_hi.astype(F32)).astype(BF16)
    w_hi = w.astype(BF16)
    w_lo = (w - w_hi.astype(F32)).astype(BF16)
    logits = (jnp.dot(h_hi, w_hi, preferred_element_type=F32)
              + jnp.dot(h_hi, w_lo, preferred_element_type=F32)
              + jnp.dot(h_lo, w_hi, preferred_element_type=F32)) + rb_ref[...]

    lane = lax.broadcasted_iota(jnp.int32, (tm, LANES), 1).astype(F32)
    work = logits
    vals, idxs = [], []
    chosen = jnp.zeros((tm, LANES), F32)
    for _ in range(TOP_K):
        m = jnp.max(work, axis=1, keepdims=True)
        am = jnp.min(jnp.where(work == m, lane, float(LANES)), axis=1, keepdims=True)
        hit = lane == am
        vals.append(m)
        idxs.append(am)
        chosen = jnp.where(hit, 1.0, chosen)
        work = jnp.where(hit, -jnp.inf, work)
    es = [jnp.exp(v - vals[0]) for v in vals]
    denom = es[0] + es[1] + es[2] + es[3]

    r = lax.broadcasted_iota(jnp.int32, (tm, tm), 0)
    cidx = lax.broadcasted_iota(jnp.int32, (tm, tm), 1)
    lstrict = jnp.where(cidx < r, 1.0, 0.0).astype(BF16)
    before_tile = jnp.dot(lstrict, chosen.astype(BF16), preferred_element_type=F32)

    tile_cnt = jnp.sum(chosen, axis=0, keepdims=True)
    cnt_hi = jnp.floor(tile_cnt * (1.0 / 16.0))
    cnt_lo = tile_cnt - 16.0 * cnt_hi
    er = lax.broadcasted_iota(jnp.int32, (LANES, LANES), 0)
    ec = lax.broadcasted_iota(jnp.int32, (LANES, LANES), 1)
    below = jnp.where(er < ec, 1.0, 0.0).astype(BF16)
    bc = lambda v: jnp.broadcast_to(v, (SUBLANES, LANES)).astype(BF16)
    seg_start = (16.0 * jnp.dot(bc(cnt_hi), below, preferred_element_type=F32)
                 + jnp.dot(bc(cnt_lo), below, preferred_element_type=F32))[0:1, :]
    local = before_tile + seg_start

    gate_out = jnp.zeros((tm, LANES), F32)
    lpos_out = jnp.zeros((tm, LANES), jnp.int32)
    for k in range(TOP_K):
        lp = jnp.sum(jnp.where(lane == idxs[k], local, 0.0), axis=1, keepdims=True)
        gate_out = jnp.where(lane == k, es[k] / denom, gate_out)
        lpos_out = jnp.where(lane == k, lp.astype(jnp.int32), lpos_out)
    gate_ref[...] = gate_out
    lpos_ref[...] = lpos_out
    tcnt_ref[...] = jnp.broadcast_to(tile_cnt, tcnt_ref.shape).astype(jnp.int32)


def _router(x, g, scale, shift, router_w, router_b, seq):
    t, d = x.shape
    bsz = scale.shape[0]
    tm = ROUTE_TM
    pad = LANES - N_EXPERTS
    rw = jnp.pad(router_w, ((0, 0), (0, pad)))
    rb = jnp.pad(router_b, (0, pad), constant_values=-1e30).reshape(1, LANES)
    return pl.pallas_call(
        _router_kernel,
        grid=(t // tm,),
        in_specs=[
            pl.BlockSpec((tm, d), lambda i: (i, 0)),
            pl.BlockSpec((1, d), lambda i: (0, 0)),
            pl.BlockSpec((None, 1, d), lambda i: (i * tm // seq, 0, 0)),
            pl.BlockSpec((None, 1, d), lambda i: (i * tm // seq, 0, 0)),
            pl.BlockSpec((d, LANES), lambda i: (0, 0)),
            pl.BlockSpec((1, LANES), lambda i: (0, 0)),
        ],
        out_specs=[
            pl.BlockSpec((tm * ROW_TILES, LANES), lambda i: (i, 0)),
            pl.BlockSpec((tm, LANES), lambda i: (i, 0)),
            pl.BlockSpec((tm, LANES), lambda i: (i, 0)),
            pl.BlockSpec((SUBLANES, LANES), lambda i: (i, 0)),
        ],
        out_shape=[
            jax.ShapeDtypeStruct((t * ROW_TILES, LANES), F32),
            jax.ShapeDtypeStruct((t, LANES), F32),
            jax.ShapeDtypeStruct((t, LANES), jnp.int32),
            jax.ShapeDtypeStruct((t // tm * SUBLANES, LANES), jnp.int32),
        ],
        compiler_params=_cparams(("parallel",)),
        name="moe_router",
    )(x, g.reshape(1, d), scale.reshape(bsz, 1, d), shift.reshape(bsz, 1, d), rw, rb)


def _segment_copies(tile, cnt_ref, start_ref, base_ref, make_copy):
    top_bit = ROUTE_TM.bit_length() - 1

    def segment(e, carry):
        n = cnt_ref[tile * N_EXPERTS + e]
        local = start_ref[tile * N_EXPERTS + e]
        glob = base_ref[tile * N_EXPERTS + e]
        for b in range(top_bit, -1, -1):
            done = (n >> (b + 1)) << (b + 1)

            @pl.when(((n >> b) & 1) == 1)
            def _():
                make_copy(pl.multiple_of((local + done) * ROW_TILES, ROW_TILES),
                          pl.multiple_of((glob + done) * ROW_TILES, ROW_TILES),
                          (1 << b) * ROW_TILES).start(priority=b % 2)
        return carry

    lax.fori_loop(0, N_EXPERTS, segment, 0)


def _segment_tables(tile_cnt, off):
    n_tiles = tile_cnt.shape[0]
    zeros_col = jnp.zeros((n_tiles, 1), jnp.int32)
    seg_start = jnp.concatenate([zeros_col, jnp.cumsum(tile_cnt, axis=1)[:, :-1]], axis=1)
    before = jnp.concatenate(
        [jnp.zeros((1, N_EXPERTS), jnp.int32), jnp.cumsum(tile_cnt, axis=0)[:-1]], axis=0)
    base = off[None, :N_EXPERTS] + before
    flat = lambda a: a.astype(jnp.int32).reshape(n_tiles * N_EXPERTS)
    return flat(tile_cnt), flat(seg_start), flat(base)


def _dispatch_kernel(cnt_ref, start_ref, base_ref, lpos_ref, h_ref, xs_ref, loc_ref, sems):
    tm = h_ref.shape[0] // ROW_TILES
    i = pl.program_id(0)
    slot = i % 2
    slot_rows = tm * TOP_K * ROW_TILES
    unroll = 4

    def wait_slot(s):
        pltpu.make_async_copy(loc_ref.at[s], xs_ref.at[pl.ds(0, slot_rows)], sems.at[s]).wait()

    def place(j, carry):
        for u in range(unroll):
            t = j * unroll + u
            v = h_ref[pl.ds(pl.multiple_of(t * ROW_TILES, ROW_TILES), ROW_TILES), :]
            for k in range(TOP_K):
                row = pl.multiple_of(lpos_ref[t * TOP_K + k], ROW_TILES)
                loc_ref[slot, pl.ds(row, ROW_TILES), :] = v
        return carry

    lax.fori_loop(0, tm // unroll, place, 0)

    _segment_copies(i, cnt_ref, start_ref, base_ref,
                    lambda loc_row, glob_row, size: pltpu.make_async_copy(
                        loc_ref.at[slot, pl.ds(loc_row, size)], xs_ref.at[pl.ds(glob_row, size)],
                        sems.at[slot]))

    @pl.when(i > 0)
    def _():
        wait_slot(1 - slot)

    @pl.when(i == pl.num_programs(0) - 1)
    def _():
        wait_slot(slot)


def _dispatch(h_tiles, lpos_flat, tables):
    rows = h_tiles.shape[0]
    t = rows // ROW_TILES
    tm = ROUTE_TM
    n_tiles = t // tm
    grid_spec = pltpu.PrefetchScalarGridSpec(
        num_scalar_prefetch=3,
        grid=(n_tiles,),
        in_specs=[
            pl.BlockSpec((tm * TOP_K,), lambda i, c, s, b: (i,), memory_space=pltpu.SMEM),
            pl.BlockSpec((tm * ROW_TILES, LANES), lambda i, c, s, b: (i, 0)),
        ],
        out_specs=pl.BlockSpec(memory_space=pl.ANY),
        scratch_shapes=[
            pltpu.VMEM((2, tm * TOP_K * ROW_TILES, LANES), F32),
            pltpu.SemaphoreType.DMA((2,)),
        ],
    )
    return pl.pallas_call(
        _dispatch_kernel,
        grid_spec=grid_spec,
        out_shape=jax.ShapeDtypeStruct((rows * TOP_K, LANES), F32),
        compiler_params=_cparams(("arbitrary",)),
        name="moe_dispatch",
    )(*tables, lpos_flat, h_tiles)


def _ffn_kernel(tile_ref, exp_ref, lo_ref, hi_ref, first_ref, newexp_ref,
                xs_ref, win_ref, bg_ref, bu_ref, wout_ref, bo_ref, perm_ref, ys_ref,
                wg_ref, wu_ref, wo_ref):
    del tile_ref, exp_ref
    i = pl.program_id(0)
    tm = xs_ref.shape[0] // ROW_TILES
    lo, hi = lo_ref[i], hi_ref[i]

    @pl.when(newexp_ref[i] == 1)
    def _():
        pw = 2 * LANES
        for cb in range(win_ref.shape[1] // pw):
            blk = win_ref[:, cb * pw:(cb + 1) * pw].astype(BF16)
            res = jnp.dot(blk, perm_ref[...], preferred_element_type=F32).astype(BF16)
            wg_ref[:, cb * LANES:(cb + 1) * LANES] = res[:, :LANES]
            wu_ref[:, cb * LANES:(cb + 1) * LANES] = res[:, LANES:]
        wo_ref[...] = wout_ref[...].astype(BF16)

    def compute():
        x = jnp.concatenate(
            [xs_ref[pl.ds(s, tm, stride=ROW_TILES), :] for s in range(ROW_TILES)], axis=1).astype(BF16)
        gate = jnp.dot(x, wg_ref[...], preferred_element_type=F32) + bg_ref[...]
        up = jnp.dot(x, wu_ref[...], preferred_element_type=F32) + bu_ref[...]
        gate = jnp.minimum(gate, SWIGLU_LIMIT)
        up = jnp.clip(up, -SWIGLU_LIMIT, SWIGLU_LIMIT)
        act = (up + 1.0) * gate * _sigmoid(SWIGLU_ALPHA * gate)
        y = jnp.dot(act.astype(BF16), wo_ref[...], preferred_element_type=F32) + bo_ref[...]
        r = lax.broadcasted_iota(jnp.int32, (tm, 1), 0)
        return jnp.where(jnp.logical_and(r >= lo, r < hi), y, 0.0)

    @pl.when(first_ref[i] == 1)
    def _():
        y = compute()
        for s in range(ROW_TILES):
            ys_ref[pl.ds(s, tm, stride=ROW_TILES), :] = y[:, s * LANES:(s + 1) * LANES]

    @pl.when(jnp.logical_and(first_ref[i] == 0, hi > lo))
    def _():
        y = compute()
        for s in range(ROW_TILES):
            ys_ref[pl.ds(s, tm, stride=ROW_TILES), :] += y[:, s * LANES:(s + 1) * LANES]


def _ffn(xs_tiles, items, w_in, bg, bu, w_out, bo, layer, tm):
    rows = xs_tiles.shape[0]
    n_items = items[0].shape[0]
    d, f2 = w_in.shape[2], w_in.shape[3]
    f = f2 // 2
    pw = 2 * LANES
    src = jnp.arange(pw)[:, None]
    dst = jnp.arange(pw)[None, :]
    perm = (src == jnp.where(dst < LANES, 2 * dst, 2 * (dst - LANES) + 1)).astype(BF16)
    wspec = lambda shape: pl.BlockSpec(
        shape, lambda i, tile, ex, lo, hi, fi, ne: (layer, ex[i], 0, 0))
    tspec = pl.BlockSpec((tm * ROW_TILES, LANES), lambda i, tile, ex, lo, hi, fi, ne: (tile[i], 0))
    grid_spec = pltpu.PrefetchScalarGridSpec(
        num_scalar_prefetch=6,
        grid=(n_items,),
        in_specs=[
            tspec,
            wspec((None, None, d, f2)),
            wspec((None, None, 1, f)),
            wspec((None, None, 1, f)),
            wspec((None, None, f, d)),
            wspec((None, None, 1, d)),
            pl.BlockSpec((pw, pw), lambda i, tile, ex, lo, hi, fi, ne: (0, 0)),
        ],
        out_specs=tspec,
        scratch_shapes=[
            pltpu.VMEM((d, f), BF16),
            pltpu.VMEM((d, f), BF16),
            pltpu.VMEM((f, d), BF16),
        ],
    )
    return pl.pallas_call(
        _ffn_kernel,
        grid_spec=grid_spec,
        out_shape=jax.ShapeDtypeStruct((rows, LANES), F32),
        compiler_params=_cparams(("arbitrary",)),
        name="moe_ffn",
    )(*items, xs_tiles, w_in, bg, bu, w_out, bo, perm)


def _combine_kernel(cnt_ref, start_ref, base_ref, lpos_ref, gates_ref, x_ref, gf_ref, pg_ref,
                    ys_ref, o_ref, loc_ref, moe_ref, sems):
    tm = x_ref.shape[0]
    i = pl.program_id(0)
    slot = i % 2
    slot_rows = tm * TOP_K * ROW_TILES
    unroll = 4

    def fetch(tile, s):
        _segment_copies(tile, cnt_ref, start_ref, base_ref,
                        lambda loc_row, glob_row, size: pltpu.make_async_copy(
                            ys_ref.at[pl.ds(glob_row, size)], loc_ref.at[s, pl.ds(loc_row, size)],
                            sems.at[s]))

    @pl.when(i == 0)
    def _():
        fetch(i, 0)

    @pl.when(i + 1 < pl.num_programs(0))
    def _():
        fetch(i + 1, 1 - slot)

    pltpu.make_async_copy(ys_ref.at[pl.ds(0, slot_rows)], loc_ref.at[slot], sems.at[slot]).wait()

    def mix(j, carry):
        for u in range(unroll):
            t = j * unroll + u
            acc = None
            for k in range(TOP_K):
                row = pl.multiple_of(lpos_ref[t * TOP_K + k], ROW_TILES)
                term = gates_ref[t * TOP_K + k] * loc_ref[slot, pl.ds(row, ROW_TILES), :]
                acc = term if acc is None else acc + term
            moe_ref[pl.ds(pl.multiple_of(t * ROW_TILES, ROW_TILES), ROW_TILES), :] = acc
        return carry

    lax.fori_loop(0, tm // unroll, mix, 0)

    moe = jnp.concatenate(
        [moe_ref[pl.ds(s, tm, stride=ROW_TILES), :] for s in range(ROW_TILES)], axis=1)
    o_ref[...] = x_ref[...] + gf_ref[...] * (_rms(moe) * pg_ref[...])


def _combine(ys_tiles, lpos_flat, gates_flat, tables, x, gate_f, post_g, seq):
    t, d = x.shape
    bsz = gate_f.shape[0]
    tm = ROUTE_TM
    grid_spec = pltpu.PrefetchScalarGridSpec(
        num_scalar_prefetch=3,
        grid=(t // tm,),
        in_specs=[
            pl.BlockSpec((tm * TOP_K,), lambda i, c, s, b: (i,), memory_space=pltpu.SMEM),
            pl.BlockSpec((tm * TOP_K,), lambda i, c, s, b: (i,), memory_space=pltpu.SMEM),
            pl.BlockSpec((tm, d), lambda i, c, s, b: (i, 0)),
            pl.BlockSpec((None, 1, d), lambda i, c, s, b: (i * tm // seq, 0, 0)),
            pl.BlockSpec((1, d), lambda i, c, s, b: (0, 0)),
            pl.BlockSpec(memory_space=pl.ANY),
        ],
        out_specs=pl.BlockSpec((tm, d), lambda i, c, s, b: (i, 0)),
        scratch_shapes=[
            pltpu.VMEM((2, tm * TOP_K * ROW_TILES, LANES), F32),
            pltpu.VMEM((tm * ROW_TILES, LANES), F32),
            pltpu.SemaphoreType.DMA((2,)),
        ],
    )
    return pl.pallas_call(
        _combine_kernel,
        grid_spec=grid_spec,
        out_shape=jax.ShapeDtypeStruct((t, d), F32),
        compiler_params=_cparams(("arbitrary",)),
        name="moe_combine",
    )(*tables, lpos_flat, gates_flat, x, gate_f.reshape(bsz, 1, d), post_g.reshape(1, d), ys_tiles)


def _ffn_items(counts, n_rows, tm):
    n_tiles = n_rows // tm
    n_items = n_tiles + N_EXPERTS - 1
    off = jnp.concatenate([jnp.zeros((1,), jnp.int32), jnp.cumsum(counts).astype(jnp.int32)])
    first_tile = off[:-1] // tm
    last_tile = (off[1:] - 1) // tm
    per = jnp.where(counts > 0, last_tile - first_tile + 1, 0)
    istart = jnp.concatenate([jnp.zeros((1,), jnp.int32), jnp.cumsum(per).astype(jnp.int32)])
    total = istart[-1]
    i = jnp.arange(n_items, dtype=jnp.int32)
    e = jnp.sum((istart[None, :] <= i[:, None]).astype(jnp.int32), axis=1) - 1
    e = jnp.clip(e, 0, N_EXPERTS - 1)
    valid = i < total
    e = jnp.where(valid, e, e[jnp.maximum(total - 1, 0)])
    tile = jnp.where(valid, first_tile[e] + (i - istart[e]), n_tiles - 1)
    lo = jnp.where(valid, jnp.clip(off[e] - tile * tm, 0, tm), 0)
    hi = jnp.where(valid, jnp.clip(off[e + 1] - tile * tm, 0, tm), 0)
    prev_tile = jnp.concatenate([jnp.full((1,), -1, jnp.int32), tile[:-1]])
    first = (tile != prev_tile).astype(jnp.int32)
    prev_e = jnp.concatenate([jnp.full((1,), -1, jnp.int32), e[:-1]])
    new_expert = (e != prev_e).astype(jnp.int32)
    return off, (tile.astype(jnp.int32), e.astype(jnp.int32), lo.astype(jnp.int32),
                 hi.astype(jnp.int32), first, new_expert)


def _moe_layer(x, pre_g, scale, shift, gate_f, post_g, router_w, router_b,
               w_in, b_in, w_out, b_out, layer, seq):
    t = x.shape[0]
    depth = w_in.shape[0]
    ffn_tm = 512
    h_tiles, gates, lpos, tcnt = _router(x, pre_g, scale, shift, router_w, router_b, seq)
    tile_cnt = tcnt.reshape(t // ROUTE_TM, SUBLANES, LANES)[:, 0, :N_EXPERTS]
    counts = jnp.sum(tile_cnt, axis=0)
    off, items = _ffn_items(counts, t * TOP_K, ffn_tm)
    tables = _segment_tables(tile_cnt, off)
    lpos_flat = (lpos[:, :TOP_K] * ROW_TILES).reshape(t * TOP_K)
    xs = _dispatch(h_tiles, lpos_flat, tables)
    bg = b_in[:, :, 0::2].reshape(depth, N_EXPERTS, 1, -1)
    bu = b_in[:, :, 1::2].reshape(depth, N_EXPERTS, 1, -1)
    ys = _ffn(xs, items, w_in, bg, bu, w_out, b_out.reshape(depth, N_EXPERTS, 1, -1), layer, ffn_tm)
    return _combine(ys, lpos_flat, gates[:, :TOP_K].reshape(t * TOP_K), tables, x, gate_f, post_g, seq)


def kernel(x, c, ada_w, ada_b, mix_pre_g, mix_post_g, ffn_pre_g, ffn_post_g, ssd_in_w, ssd_conv_w,
           ssd_conv_b, ssd_dt_bias, ssd_a_log, ssd_d, ssd_norm_g, ssd_out_w, kv_norm_g, kv_w, sb_q_w,
           sb_o_w, router_w, router_b, exp_w_in, exp_b_in, exp_w_out, exp_b_out):
    bsz, seq, d = x.shape
    t = bsz * seq
    xf = x.reshape(t, d)
    mod = _ada_mod(c, ada_w, ada_b)

    sh_m, sc_m, g_m, sh_f, sc_f, g_f = [mod[0, :, k * d:(k + 1) * d] for k in range(6)]
    in_w = ssd_in_w[0]
    w_z = in_w[:, :D_INNER].astype(BF16)
    w_xbc = in_w[:, D_INNER:D_INNER + CONV_DIM].astype(BF16)
    w_dt = jnp.pad(in_w[:, D_INNER + CONV_DIM:], ((0, 0), (0, LANES - SSD_HEADS))).astype(BF16)
    z, xbc, dt_raw = _norm_mm(xf, mix_pre_g[0], sc_m, sh_m, [w_z, w_xbc, w_dt],
                              [BF16, BF16, F32], seq, "ssd_in_proj")
    y = _ssd(z, xbc, dt_raw, ssd_conv_w[0], ssd_conv_b[0], ssd_dt_bias[0], ssd_a_log[0],
             ssd_d[0], ssd_norm_g[0], bsz, seq)
    xf = _mm_resid(y, ssd_out_w[0].astype(BF16), xf, g_m, mix_post_g[0], seq, "ssd_out_proj")
    xf = _moe_layer(xf, ffn_pre_g[0], sc_f, sh_f, g_f, ffn_post_g[0], router_w[0], router_b[0],
                    exp_w_in, exp_b_in, exp_w_out, exp_b_out, 0, seq)

    sh_m, sc_m, g_m, sh_f, sc_f, g_f = [mod[1, :, k * d:(k + 1) * d] for k in range(6)]
    kv, q = _kvq_proj(xf, kv_norm_g, mix_pre_g[1], sc_m, sh_m, kv_w.astype(BF16),
                      sb_q_w[0].astype(BF16), seq)
    att = _attention(q, kv, bsz, seq)
    xf = _mm_resid(att, sb_o_w[0].astype(BF16), xf, g_m, mix_post_g[1], seq, "attn_out_proj")
    xf = _moe_layer(xf, ffn_pre_g[1], sc_f, sh_f, g_f, ffn_post_g[1], router_w[1], router_b[1],
                    exp_w_in, exp_b_in, exp_w_out, exp_b_out, 1, seq)
    return xf.reshape(bsz, seq, d)
```
